```python
import jax, jax.numpy as jnp
from jax import lax
import numpy as np

D_MODEL = 1024
BATCH = 32
SEQ = 2048
DEPTH = 2
DEC_BATCH = 8
DEC_SEQ = 64
PAST_LEN = 1024

CHUNK = 64
A_HEADS = 8
A_DK = 128
A_DV = 128
D_A = A_HEADS * A_DK
D_B = D_MODEL // 2
CONV_K = 31
N_EXPERTS = 16
N_GROUPS = 4
EXP_PER_GROUP = N_EXPERTS // N_GROUPS
TOP_K = 2
D_FF = D_MODEL
MASK_SCORE = -1.0e4
N_IN = 4 * D_A + 2 * D_B + 2 * D_MODEL
SPLITS = [D_A, 2 * D_A, 3 * D_A, 4 * D_A, 4 * D_A + D_B, 4 * D_A + 2 * D_B,
          4 * D_A + 2 * D_B + D_MODEL]
DN_ALPHA = (2 * DEPTH) ** 0.25
DN_BETA = (8 * DEPTH) ** -0.25
LN_EPS = 1e-5

kernel_name = "hgrn2_conformer_gated_moe_stream_step"


def layer_norm(x, g, b):
    xf = x.astype(jnp.float32)
    mu = jnp.mean(xf, -1, keepdims=True)
    var = jnp.mean(jnp.square(xf - mu), -1, keepdims=True)
    return ((xf - mu) * lax.rsqrt(var + LN_EPS) * g + b).astype(x.dtype)


def rms_norm(x, g):
    xf = x.astype(jnp.float32)
    return xf * lax.rsqrt(jnp.mean(xf * xf, -1, keepdims=True) + LN_EPS) * g


def hgrn_lower_bounds(lb_logits):
    p = jax.nn.softmax(lb_logits.astype(jnp.float32), axis=0)
    return jnp.cumsum(p, axis=0) - p[0]


def gla_block(S, q, k, v, logf):
    L = q.shape[2]
    b = jnp.cumsum(logf, axis=2)
    causal = jnp.tril(jnp.ones((L, L), dtype=jnp.float32))[None, None, :, :, None]
    diff = b[:, :, :, None, :] - b[:, :, None, :, :]
    decay = jnp.exp(jnp.minimum(diff, 0.0)) * causal
    attn = jnp.einsum('bhtd,bhsd,bhtsd->bhts', q, k, decay)
    o = (jnp.einsum('bhts,bhsv->bhtv', attn, v)
         + jnp.einsum('bhtd,bhdv->bhtv', q * jnp.exp(b), S))
    b_last = b[:, :, -1:, :]
    k_dec = k * jnp.exp(b_last - b)
    S_new = (jnp.exp(b_last[:, :, 0, :])[..., None] * S
             + jnp.einsum('bhsd,bhsv->bhdv', k_dec, v))
    return S_new, o


def hgrn_scan(S0, q, k, v, logf):
    B, H, T, _ = q.shape
    n = T // CHUNK

    def to_blocks(a):
        return a.reshape(B, H, n, CHUNK, a.shape[-1]).transpose(2, 0, 1, 3, 4)

    def step(S, blk):
        return gla_block(S, *blk)

    S, o = lax.scan(step, S0, (to_blocks(q), to_blocks(k), to_blocks(v), to_blocks(logf)))
    o = o.transpose(1, 2, 0, 3, 4).reshape(B, H, T, -1)
    return S, o


def token_mixer(h, hg_state, conv_hist, lb, blocked, w_in, hg_norm_g, w_o_a,
                conv_w, conv_b, cn_g, cn_b, w_o_b, w_out):
    B, T, _ = h.shape
    f32 = jnp.float32
    proj = jnp.einsum('btd,dn->btn', h, w_in)
    q, f_pre, i_v, g_o, u_a, u_b, gate_a, gate_b = jnp.split(proj, SPLITS, axis=-1)

    zf = f_pre.astype(f32)
    lb = lb.astype(f32)
    logf = jax.nn.log_sigmoid(zf) + jnp.log1p(lb * jnp.exp(-zf))
    k = (1.0 - lb) * jax.nn.sigmoid(-zf)

    def heads(a):
        return a.astype(f32).reshape(B, T, A_HEADS, -1).transpose(0, 2, 1, 3)

    qh = heads(q) * (A_DK ** -0.5)
    kh, vh, lfh = heads(k), heads(i_v), heads(logf)
    S0 = hg_state.astype(f32)
    if blocked:
        S_new, o = hgrn_scan(S0, qh, kh, vh, lfh)
    else:
        S_new, o = gla_block(S0, qh, kh, vh, lfh)
    o = rms_norm(o, hg_norm_g).transpose(0, 2, 1, 3).reshape(B, T, D_A).astype(h.dtype)
    o = o * jax.nn.silu(g_o)
    y_a = jnp.einsum('btc,cd->btd', o, w_o_a)

    u = u_a * jax.nn.sigmoid(u_b)
    u_full = jnp.concatenate([conv_hist.astype(u.dtype), u], axis=1)
    c = lax.conv_general_dilated(
        u_full, conv_w[:, None, :].astype(u.dtype), window_strides=(1,), padding='VALID',
        dimension_numbers=('NWC', 'WIO', 'NWC'), feature_group_count=D_B) + conv_b
    c = jax.nn.silu(layer_norm(c, cn_g, cn_b))
    y_b = jnp.einsum('btc,cd->btd', c, w_o_b)
    new_conv = u_full[:, -(CONV_K - 1):, :]

    m = jax.nn.sigmoid(gate_a) * y_a + jax.nn.sigmoid(gate_b) * y_b
    out = jnp.einsum('btd,de->bte', m, w_out)
    return out, S_new, new_conv


def moe(h, w_router, router_bias, w_gate, w_up, w_down):
    B, T, _ = h.shape
    f32 = jnp.float32
    logits = jnp.einsum('btd,de->bte', h, w_router).astype(f32)
    scores = jax.nn.softmax(logits, axis=-1)
    sel = scores + router_bias.astype(f32)
    sel_g = sel.reshape(B, T, N_GROUPS, EXP_PER_GROUP)
    group_score = lax.top_k(sel_g, TOP_K)[0].sum(-1)
    g_idx = jnp.argmax(group_score, axis=-1)
    group_mask = jax.nn.one_hot(g_idx, N_GROUPS, dtype=jnp.bool_)[..., None]
    masked = jnp.where(group_mask, sel_g, MASK_SCORE).reshape(B, T, N_EXPERTS)
    _, e_idx = lax.top_k(masked, TOP_K)
    w_sel = jnp.take_along_axis(scores, e_idx, axis=-1)
    w_sel = w_sel / jnp.sum(w_sel, -1, keepdims=True)
    gates = jnp.sum(jax.nn.one_hot(e_idx, N_EXPERTS, dtype=f32) * w_sel[..., None], axis=-2)
    gates = gates.astype(h.dtype)
    y = jnp.zeros_like(h)
    for e in range(N_EXPERTS):
        a = jnp.einsum('btd,df->btf', h, w_gate[e])
        u = jnp.einsum('btd,df->btf', h, w_up[e])
        y = y + gates[..., e:e + 1] * jnp.einsum('btf,fd->btd', jax.nn.silu(a) * u, w_down[e])
    return y


def trunk(x, hg_state, conv_state, blocked, p):
    (ln_in_g, ln_in_b, w_in, lb_logits, hg_norm_g, w_o_a, conv_w, conv_b, cn_g, cn_b,
     w_o_b, w_out, ln1_g, ln1_b, w_router, router_bias, w_gate_e, w_up_e, w_down_e,
     ln2_g, ln2_b) = p
    lbs = hgrn_lower_bounds(lb_logits)
    x = layer_norm(x, ln_in_g, ln_in_b)
    new_hg, new_conv = [], []
    for l in range(DEPTH):
        mix, S_l, c_l = token_mixer(x, hg_state[l], conv_state[l], lbs[l], blocked,
                                    w_in[l], hg_norm_g[l], w_o_a[l], conv_w[l], conv_b[l],
                                    cn_g[l], cn_b[l], w_o_b[l], w_out[l])
        x = layer_norm(DN_ALPHA * x + mix, ln1_g[l], ln1_b[l])
        ff = moe(x, w_router, router_bias, w_gate_e[l], w_up_e[l], w_down_e[l])
        x = layer_norm(DN_ALPHA * x + ff, ln2_g[l], ln2_b[l])
        new_hg.append(S_l.astype(x.dtype))
        new_conv.append(c_l.astype(x.dtype))
    return x, jnp.stack(new_hg), jnp.stack(new_conv)


def setup_inputs(seed: int = 0) -> dict:
    key = jax.random.key(seed)
    ks = jax.random.split(key, 32)
    n = lambda i, s: jax.random.normal(ks[i], s, jnp.float32)
    return {
        "x_prompt": n(0, (BATCH, SEQ, D_MODEL)),
        "x_sample": n(1, (DEC_BATCH, DEC_SEQ, D_MODEL)),
        "state_hgrn": 0.3 * n(2, (DEPTH, DEC_BATCH, A_HEADS, A_DK, A_DV)),
        "state_conv": 0.5 * n(3, (DEPTH, DEC_BATCH, CONV_K - 1, D_B)),
        "ln_in_g": 1.0 + 0.05 * n(4, (D_MODEL,)),
        "ln_in_b": 0.05 * n(5, (D_MODEL,)),
        "w_in": n(6, (DEPTH, D_MODEL, N_IN)) * D_MODEL ** -0.5,
        "lb_logits": 0.5 * n(7, (DEPTH, D_A)),
        "hg_norm_g": 1.0 + 0.05 * n(8, (DEPTH, A_DV)),
        "w_o_a": n(9, (DEPTH, D_A, D_MODEL)) * (D_A ** -0.5 * DN_BETA),
        "conv_w": n(10, (DEPTH, CONV_K, D_B)) * CONV_K ** -0.5,
        "conv_b": 0.02 * n(11, (DEPTH, D_B)),
        "cn_g": 1.0 + 0.05 * n(12, (DEPTH, D_B)),
        "cn_b": 0.05 * n(13, (DEPTH, D_B)),
        "w_o_b": n(14, (DEPTH, D_B, D_MODEL)) * (D_B ** -0.5 * DN_BETA),
        "w_out": n(15, (DEPTH, D_MODEL, D_MODEL)) * (D_MODEL ** -0.5 * DN_BETA),
        "ln1_g": 1.0 + 0.05 * n(16, (DEPTH, D_MODEL)),
        "ln1_b": 0.05 * n(17, (DEPTH, D_MODEL)),
        "w_router": n(18, (D_MODEL, N_EXPERTS)) * D_MODEL ** -0.5,
        "router_bias": 0.01 * n(19, (N_EXPERTS,)),
        "w_gate_e": n(20, (DEPTH, N_EXPERTS, D_MODEL, D_FF)) * D_MODEL ** -0.5,
        "w_up_e": n(21, (DEPTH, N_EXPERTS, D_MODEL, D_FF)) * D_MODEL ** -0.5,
        "w_down_e": n(22, (DEPTH, N_EXPERTS, D_FF, D_MODEL)) * (D_FF ** -0.5 * DN_BETA),
        "ln2_g": 1.0 + 0.05 * n(23, (DEPTH, D_MODEL)),
        "ln2_b": 0.05 * n(24, (DEPTH, D_MODEL)),
    }


def reference(x_prompt, x_sample, state_hgrn, state_conv, ln_in_g, ln_in_b, w_in, lb_logits,
              hg_norm_g, w_o_a, conv_w, conv_b, cn_g, cn_b, w_o_b, w_out, ln1_g, ln1_b,
              w_router, router_bias, w_gate_e, w_up_e, w_down_e, ln2_g, ln2_b):
    p = (ln_in_g, ln_in_b, w_in, lb_logits, hg_norm_g, w_o_a, conv_w, conv_b, cn_g, cn_b,
         w_o_b, w_out, ln1_g, ln1_b, w_router, router_bias, w_gate_e, w_up_e, w_down_e,
         ln2_g, ln2_b)
    bp = x_prompt.shape[0]
    hg0 = jnp.zeros((DEPTH, bp, A_HEADS, A_DK, A_DV), x_prompt.dtype)
    cv0 = jnp.zeros((DEPTH, bp, CONV_K - 1, D_B), x_prompt.dtype)
    y_prompt, hg_prompt, conv_prompt = trunk(x_prompt, hg0, cv0, True, p)
    y_sample, hg_sample, conv_sample = trunk(x_sample, state_hgrn, state_conv, False, p)
    return (y_prompt, y_sample, hg_prompt, conv_prompt, hg_sample, conv_sample)
```

```python
import functools
import math

import jax
import jax.numpy as jnp
import numpy as np
from jax import lax
from jax.experimental import pallas as pl
from jax.experimental.pallas import tpu as pltpu

F32 = jnp.float32
BF16 = jnp.bfloat16

CONV_K = 31
N_EXPERTS = 16
N_GROUPS = 4
EXP_PER_GROUP = N_EXPERTS // N_GROUPS
MASK_SCORE = -1.0e4
LN_EPS = 1e-5
HIST_ROWS = 32
ROUTE_ROWS = 8
V7X_VMEM_LIMIT_BYTES = 60000 * 1024
MIXER_TILE = 256
HGRN_CHUNK = 128
MOE_TILE = 256


def _dot(a, b):
    return jnp.dot(a, b, preferred_element_type=F32)


def _dot_nt(a, b):
    return lax.dot_general(a, b, (((1,), (1,)), ((), ())), preferred_element_type=F32)


def _layer_norm(x, g, b):
    mu = jnp.mean(x, axis=-1, keepdims=True)
    xc = x - mu
    var = jnp.mean(xc * xc, axis=-1, keepdims=True)
    return xc * lax.rsqrt(var + LN_EPS) * g + b


def _split3(x):
    hi = x.astype(BF16)
    r1 = x - hi.astype(F32)
    mid = r1.astype(BF16)
    lo = (r1 - mid.astype(F32)).astype(BF16)
    return hi, mid, lo


def _level_weight(bc, h, rows):
    two_h = 2 * h
    if two_h >= 8:
        b3 = bc.reshape(rows // two_h, two_h, bc.shape[-1])
        ref = jnp.broadcast_to(b3[:, h - 1:h, :], b3.shape).reshape(bc.shape)
    else:
        u = lax.broadcasted_iota(jnp.int32, bc.shape, 0) & (two_h - 1)
        ref = bc
        for uu in range(two_h):
            delta = h - 1 - uu
            if delta == 0:
                continue
            shifted = pltpu.roll(bc, (-delta) % rows, 0)
            ref = jnp.where(u == uu, shifted, ref)
    return jnp.exp(-jnp.abs(bc - ref))


def _route_rows(logits_t, bias_t):
    rows = [logits_t[e:e + 1, :] for e in range(N_EXPERTS)]
    mx = functools.reduce(jnp.maximum, rows)
    ex = [jnp.exp(r - mx) for r in rows]
    den = functools.reduce(lambda a, b: a + b, ex)
    scores = [e_ / den for e_ in ex]
    sel = [scores[e] + bias_t[e:e + 1, :] for e in range(N_EXPERTS)]
    gscore = []
    for g in range(N_GROUPS):
        v = sel[g * EXP_PER_GROUP:(g + 1) * EXP_PER_GROUP]
        pairs = [v[i] + v[j] for i in range(EXP_PER_GROUP) for j in range(i + 1, EXP_PER_GROUP)]
        gscore.append(functools.reduce(jnp.maximum, pairs))
    best = gscore[0]
    gidx = jnp.zeros_like(best, dtype=jnp.int32)
    for g in range(1, N_GROUPS):
        upd = gscore[g] > best
        best = jnp.where(upd, gscore[g], best)
        gidx = jnp.where(upd, g, gidx)
    masked = [jnp.where(gidx == (e // EXP_PER_GROUP), sel[e], MASK_SCORE) for e in range(N_EXPERTS)]

    def top1(vals):
        bv = vals[0]
        bi = jnp.zeros_like(gidx)
        for e in range(1, N_EXPERTS):
            upd = vals[e] > bv
            bv = jnp.where(upd, vals[e], bv)
            bi = jnp.where(upd, e, bi)
        return bi

    i1 = top1(masked)
    i2 = top1([jnp.where(i1 == e, -jnp.inf, masked[e]) for e in range(N_EXPERTS)])
    w1 = jnp.zeros_like(best)
    w2 = jnp.zeros_like(best)
    for e in range(N_EXPERTS):
        w1 = jnp.where(i1 == e, scores[e], w1)
        w2 = jnp.where(i2 == e, scores[e], w2)
    tot = w1 + w2
    zero = jnp.zeros_like(best)
    return [i1.astype(F32), i2.astype(F32), w1 / tot, w2 / tot] + [zero] * (ROUTE_ROWS - 4)


def _mixer_kernel(x_ref, hg_ref, cv_ref, lb_ref, lnin_g, lnin_b, wq, wf, wv, wg, wua, wub, wga, wgb,
                  hgn_ref, woa, cw_ref, cb_ref, cng, cnb, wob, wout, ln1g, ln1b, wr_hi, wr_lo, rb_ref,
                  x1_ref, route_ref, hgo_ref, cvo_ref,
                  st_scr, ubuf, q_scr, k_scr, v_scr, b_scr, o_scr,
                  *, entry_ln, tile, chunk, heads, dk, alpha):
    t_id = pl.program_id(1)
    n_t = pl.num_programs(1)

    @pl.when(t_id == 0)
    def _():
        for h in range(heads):
            st_scr[h] = hg_ref[0, h].T
        ubuf[0:HIST_ROWS, :] = cv_ref[0]

    x = x_ref[...]
    if entry_ln:
        x = _layer_norm(x, lnin_g[...], lnin_b[...])
    xb = x.astype(BF16)

    lb = lb_ref[...]
    q_scr[...] = _dot(xb, wq[...]) * (dk ** -0.5)
    z = _dot(xb, wf[...])
    fgate = lb + (1.0 - lb) * jax.nn.sigmoid(z)
    k_scr[...] = (1.0 - lb) * jax.nn.sigmoid(-z)
    logf = jnp.log(fgate)
    v_scr[...] = _dot(xb, wv[...])

    r_i = lax.broadcasted_iota(jnp.int32, (chunk, chunk), 0)
    c_i = lax.broadcasted_iota(jnp.int32, (chunk, chunk), 1)
    tri = jnp.where(c_i <= r_i, 1.0, 0.0).astype(BF16)
    n_chunks = tile // chunk
    for c in range(n_chunks):
        hi, mid, lo = _split3(logf[c * chunk:(c + 1) * chunk, :])
        b_scr[c * chunk:(c + 1) * chunk, :] = (_dot(tri, hi) + _dot(tri, mid)) + _dot(tri, lo)

    xs = jnp.where(r_i > c_i, r_i ^ c_i, 0)
    n_levels = int(math.log2(chunk))
    diag = r_i == c_i

    def head_body(h, carry):
        off = pl.multiple_of(h * dk, dk)
        st = st_scr[h]
        for c in range(n_chunks):
            rs = slice(c * chunk, (c + 1) * chunk)
            qc = q_scr[rs, pl.ds(off, dk)]
            kc = k_scr[rs, pl.ds(off, dk)]
            vc = v_scr[rs, pl.ds(off, dk)]
            bc = b_scr[rs, pl.ds(off, dk)]
            vcb = vc.astype(BF16)
            o = _dot_nt((qc * jnp.exp(bc)).astype(BF16), st.astype(BF16))
            attn = jnp.where(diag, _dot_nt(qc.astype(BF16), kc.astype(BF16)), 0.0)
            for lv in range(n_levels):
                w = _level_weight(bc, 1 << lv, chunk)
                a = _dot_nt((qc * w).astype(BF16), (kc * w).astype(BF16))
                attn = jnp.where((xs >> lv) == 1, a, attn)
            o = o + _dot(attn.astype(BF16), vcb)
            bl = bc[chunk - 1:chunk, :]
            kd = (kc * jnp.exp(bl - bc)).astype(BF16)
            st = st * jnp.exp(bl) + _dot(vc.T.astype(BF16), kd)
            ms = jnp.mean(o * o, axis=-1, keepdims=True)
            o_scr[rs, pl.ds(off, dk)] = o * lax.rsqrt(ms + LN_EPS) * hgn_ref[...]
        st_scr[h] = st
        return carry

    lax.fori_loop(0, heads, head_body, 0)

    g_o = _dot(xb, wg[...])
    y_a = _dot((o_scr[...] * (g_o * jax.nn.sigmoid(g_o))).astype(BF16), woa[...])

    u = _dot(xb, wua[...]) * jax.nn.sigmoid(_dot(xb, wub[...]))
    ubuf[HIST_ROWS:HIST_ROWS + tile, :] = u
    base = HIST_ROWS - (CONV_K - 1)
    acc = ubuf[base:base + tile, :] * cw_ref[0:1, :]
    for j in range(1, CONV_K):
        acc = acc + ubuf[base + j:base + j + tile, :] * cw_ref[j:j + 1, :]
    cn = _layer_norm(acc + cb_ref[...], cng[...], cnb[...])
    y_b = _dot((cn * jax.nn.sigmoid(cn)).astype(BF16), wob[...])

    @pl.when(t_id == n_t - 1)
    def _():
        cvo_ref[0] = ubuf[tile:tile + HIST_ROWS, :]

    ubuf[0:HIST_ROWS, :] = ubuf[tile:tile + HIST_ROWS, :]

    m = jax.nn.sigmoid(_dot(xb, wga[...])) * y_a + jax.nn.sigmoid(_dot(xb, wgb[...])) * y_b
    x1 = _layer_norm(alpha * x + _dot(m.astype(BF16), wout[...]), ln1g[...], ln1b[...])
    x1_ref[...] = x1

    x_hi = x1.astype(BF16)
    x_lo = (x1 - x_hi.astype(F32)).astype(BF16)
    logits_t = (_dot_nt(wr_hi[...], x_hi) + _dot_nt(wr_lo[...], x_hi)) + _dot_nt(wr_hi[...], x_lo)
    rows = _route_rows(logits_t, rb_ref[...])
    for i, r in enumerate(rows):
        route_ref[0, i:i + 1, :] = r

    @pl.when(t_id == n_t - 1)
    def _():
        for h in range(heads):
            hgo_ref[0, h] = st_scr[h].T


def _const_spec(shape):
    nd = len(shape)
    return pl.BlockSpec(shape, lambda b, t, _n=nd: (0,) * _n, pipeline_mode=pl.Buffered(1))


def _mixer_call(x_rows, row_block0, n_seq, seq_len, hg, cv, w, *, entry_ln, alpha, merged=None,
                merged_rows=None, merged_block0=0):
    d_model = x_rows.shape[-1]
    heads, dk = hg.shape[1], hg.shape[2]
    d_a = heads * dk
    d_b = cv.shape[-1]
    tile = min(MIXER_TILE, seq_len)
    chunk = min(HGRN_CHUNK, tile)
    n_t = seq_len // tile
    assert seq_len % tile == 0 and tile % chunk == 0 and tile >= HIST_ROWS
    n_tiles = n_seq * n_t

    kern = functools.partial(_mixer_kernel, entry_ln=entry_ln, tile=tile, chunk=chunk, heads=heads,
                             dk=dk, alpha=alpha)
    x_spec = pl.BlockSpec((tile, d_model), lambda b, t: (row_block0 + b * n_t + t, 0))
    in_specs = [
        x_spec,
        pl.BlockSpec((1, heads, dk, dk), lambda b, t: (b, 0, 0, 0)),
        pl.BlockSpec((1, HIST_ROWS, d_b), lambda b, t: (b, 0, 0)),
    ]
    consts = [w["lb"], w["lnin_g"], w["lnin_b"], w["wq"], w["wf"], w["wv"], w["wg"], w["wua"], w["wub"],
              w["wga"], w["wgb"], w["hgn"], w["woa"], w["cw"], w["cb"], w["cng"], w["cnb"], w["wob"],
              w["wout"], w["ln1g"], w["ln1b"], w["wr_hi"], w["wr_lo"], w["rb"]]
    in_specs += [_const_spec(c.shape) for c in consts]
    args = [x_rows, hg, cv] + consts

    out_shape = [
        jax.ShapeDtypeStruct((merged_rows, d_model), F32),
        jax.ShapeDtypeStruct((n_tiles, ROUTE_ROWS, tile), F32),
        jax.ShapeDtypeStruct(hg.shape, F32),
        jax.ShapeDtypeStruct(cv.shape, F32),
    ]
    out_specs = [
        pl.BlockSpec((tile, d_model), lambda b, t: (merged_block0 + b * n_t + t, 0)),
        pl.BlockSpec((1, ROUTE_ROWS, tile), lambda b, t: (b * n_t + t, 0, 0)),
        pl.BlockSpec((1, heads, dk, dk), lambda b, t: (b, 0, 0, 0)),
        pl.BlockSpec((1, HIST_ROWS, d_b), lambda b, t: (b, 0, 0)),
    ]
    io_alias = {}
    if merged is not None:
        in_specs.append(pl.BlockSpec(memory_space=pl.ANY))
        args.append(merged)
        io_alias = {len(args) - 1: 0}
        kern = functools.partial(_drop_arg, kern, len(args) - 1)

    scratch = [
        pltpu.VMEM((heads, dk, dk), F32),
        pltpu.VMEM((HIST_ROWS + tile, d_b), F32),
        pltpu.VMEM((tile, d_a), F32),
        pltpu.VMEM((tile, d_a), F32),
        pltpu.VMEM((tile, d_a), F32),
        pltpu.VMEM((tile, d_a), F32),
        pltpu.VMEM((tile, d_a), F32),
    ]
    return pl.pallas_call(
        kern,
        grid=(n_seq, n_t),
        in_specs=in_specs,
        out_specs=out_specs,
        out_shape=out_shape,
        scratch_shapes=scratch,
        input_output_aliases=io_alias,
        compiler_params=pltpu.CompilerParams(
            dimension_semantics=("arbitrary", "arbitrary"),
            vmem_limit_bytes=V7X_VMEM_LIMIT_BYTES),
        name="mixer",
    )(*args)


def _drop_arg(kern, pos, *refs):
    return kern(*refs[:pos], *refs[pos + 1:])


def _moe_kernel(meta_ref, idx_hbm, x_hbm, gates_ref, wg1, wu1, wd1, wg2, wu2, wd2, ln2g, ln2b,
                out_hbm, idx_smem, xbuf, ybuf, isem, gsem, ssem, *, tile, alpha):
    i = pl.program_id(0)
    n_used = meta_ref[0]
    slot = i % 2
    nslot = 1 - slot

    def idx_copy(t, s):
        return pltpu.make_async_copy(idx_hbm.at[t], idx_smem.at[s], isem.at[s])

    def row_in(s, r):
        tok = idx_smem[s, r]
        return pltpu.make_async_copy(x_hbm.at[pl.ds(tok, 1)], xbuf.at[s, pl.ds(r, 1)], gsem.at[s])

    def row_out(s, r):
        tok = idx_smem[s, tile + r]
        return pltpu.make_async_copy(ybuf.at[s, pl.ds(r, 1)], out_hbm.at[pl.ds(tok, 1)], ssem.at[s])

    def start_rows(make, s):
        def body(r, c):
            make(s, r).start()
            return c
        lax.fori_loop(0, tile, body, 0, unroll=8)

    def wait_rows_in(s):
        pltpu.make_async_copy(x_hbm.at[pl.ds(0, tile)], xbuf.at[s], gsem.at[s]).wait()

    def wait_rows_out(s):
        pltpu.make_async_copy(ybuf.at[s], out_hbm.at[pl.ds(0, tile)], ssem.at[s]).wait()

    @pl.when(i == 0)
    def _():
        idx_copy(0, 0).start()
        idx_copy(0, 0).wait()
        start_rows(row_in, 0)

        @pl.when(n_used > 1)
        def _():
            idx_copy(1, 1).start()

    @pl.when(i < n_used)
    def _():
        @pl.when(i + 1 < n_used)
        def _():
            idx_copy(i + 1, nslot).wait()
            start_rows(row_in, nslot)

        wait_rows_in(slot)

        @pl.when(i >= 2)
        def _():
            wait_rows_out(slot)

        x = xbuf[slot]
        xb = x.astype(BF16)
        g = gates_ref[...]
        h1 = _dot(xb, wg1[0])
        h1 = (h1 * jax.nn.sigmoid(h1)) * _dot(xb, wu1[0])
        y = g[:, 0:1] * _dot(h1.astype(BF16), wd1[0])
        h2 = _dot(xb, wg2[0])
        h2 = (h2 * jax.nn.sigmoid(h2)) * _dot(xb, wu2[0])
        y = y + g[:, 1:2] * _dot(h2.astype(BF16), wd2[0])
        ybuf[slot] = _layer_norm(alpha * x + y, ln2g[...], ln2b[...])

        start_rows(row_out, slot)

        @pl.when(i + 2 < n_used)
        def _():
            idx_copy(i + 2, slot).start()

        @pl.when(i == n_used - 1)
        def _():
            wait_rows_out(slot)

            @pl.when(i >= 1)
            def _():
                wait_rows_out(nslot)


def _moe_call(x_merged, n_tokens, route, w, *, alpha):
    d_model = x_merged.shape[-1]
    tile = MOE_TILE
    n_pairs = N_EXPERTS * (N_EXPERTS - 1) // 2
    n_tiles = (n_tokens + n_pairs * (tile - 1)) // tile + 1
    n_slots = n_tiles * tile

    e1 = route[0].astype(jnp.int32)
    e2 = route[1].astype(jnp.int32)
    e_lo = jnp.minimum(e1, e2)
    e_hi = jnp.maximum(e1, e2)
    g_lo = jnp.where(e1 < e2, route[2], route[3])
    g_hi = jnp.where(e1 < e2, route[3], route[2])
    key = e_lo * N_EXPERTS + e_hi
    n_keys = N_EXPERTS * N_EXPERTS
    order = jnp.argsort(key).astype(jnp.int32)
    skey = key[order]
    counts = jnp.zeros((n_keys,), jnp.int32).at[key].add(1)
    starts = jnp.cumsum(counts) - counts
    padded = ((counts + tile - 1) // tile) * tile
    pends = jnp.cumsum(padded)
    pstarts = pends - padded
    pos = jnp.arange(n_tokens, dtype=jnp.int32)
    slot_of = pstarts[skey] + (pos - starts[skey])
    n_used = (pends[-1] // tile).astype(jnp.int32)

    slot_ids = jnp.arange(n_slots, dtype=jnp.int32)
    dump = n_tokens + ((slot_ids // tile) % 2) * tile + (slot_ids % tile)
    dst = dump.at[slot_of].set(order)
    src = jnp.zeros((n_slots,), jnp.int32).at[slot_of].set(order)
    idx = jnp.concatenate([src.reshape(n_tiles, tile), dst.reshape(n_tiles, tile)], axis=1)

    gl = jnp.zeros((n_slots,), F32).at[slot_of].set(g_lo[order])
    gh = jnp.zeros((n_slots,), F32).at[slot_of].set(g_hi[order])
    gates = jnp.zeros((n_slots, 128), F32).at[:, 0].set(gl).at[:, 1].set(gh)

    tile_ids = jnp.arange(n_tiles, dtype=jnp.int32)
    tkey = jnp.searchsorted(pends, tile_ids * tile, side="right").astype(jnp.int32)
    last_key = jnp.take(tkey, jnp.maximum(n_used - 1, 0))
    tkey = jnp.where(tile_ids < n_used, jnp.minimum(tkey, n_keys - 1), last_key)
    meta = jnp.concatenate([n_used[None], tkey // N_EXPERTS, tkey % N_EXPERTS]).astype(jnp.int32)

    def w_spec(which):
        def imap(i, meta_ref):
            return (meta_ref[1 + which * n_tiles + i], 0, 0)
        return imap

    wshape_up = (1,) + w["wg"].shape[1:]
    wshape_dn = (1,) + w["wd"].shape[1:]
    grid_spec = pltpu.PrefetchScalarGridSpec(
        num_scalar_prefetch=1,
        grid=(n_tiles,),
        in_specs=[
            pl.BlockSpec(memory_space=pl.ANY),
            pl.BlockSpec(memory_space=pl.ANY),
            pl.BlockSpec((tile, 128), lambda i, m: (i, 0)),
            pl.BlockSpec(wshape_up, w_spec(0)),
            pl.BlockSpec(wshape_up, w_spec(0)),
            pl.BlockSpec(wshape_dn, w_spec(0)),
            pl.BlockSpec(wshape_up, w_spec(1)),
            pl.BlockSpec(wshape_up, w_spec(1)),
            pl.BlockSpec(wshape_dn, w_spec(1)),
            pl.BlockSpec((1, d_model), lambda i, m: (0, 0)),
            pl.BlockSpec((1, d_model), lambda i, m: (0, 0)),
        ],
        out_specs=pl.BlockSpec(memory_space=pl.ANY),
        scratch_shapes=[
            pltpu.SMEM((2, 2 * tile), jnp.int32),
            pltpu.VMEM((2, tile, d_model), F32),
            pltpu.VMEM((2, tile, d_model), F32),
            pltpu.SemaphoreType.DMA((2,)),
            pltpu.SemaphoreType.DMA((2,)),
            pltpu.SemaphoreType.DMA((2,)),
        ],
    )
    return pl.pallas_call(
        functools.partial(_moe_kernel, tile=tile, alpha=alpha),
        grid_spec=grid_spec,
        out_shape=jax.ShapeDtypeStruct(x_merged.shape, F32),
        compiler_params=pltpu.CompilerParams(
            dimension_semantics=("arbitrary",),
            vmem_limit_bytes=V7X_VMEM_LIMIT_BYTES),
        name="moe",
    )(meta, idx, x_merged, gates, w["wg"], w["wu"], w["wd"], w["wg"], w["wu"], w["wd"], w["ln2g"], w["ln2b"])


def kernel(x_prompt, x_sample, state_hgrn, state_conv, ln_in_g, ln_in_b, w_in, lb_logits, hg_norm_g, w_o_a,
           conv_w, conv_b, cn_g, cn_b, w_o_b, w_out, ln1_g, ln1_b, w_router, router_bias, w_gate_e, w_up_e,
           w_down_e, ln2_g, ln2_b):
    depth = w_in.shape[0]
    bp, tp, d_model = x_prompt.shape
    bs, ts, _ = x_sample.shape
    heads, dk = state_hgrn.shape[2], state_hgrn.shape[3]
    d_a = heads * dk
    d_b = state_conv.shape[-1]
    alpha = float((2 * depth) ** 0.25)
    n_p, n_s = bp * tp, bs * ts
    n_tok = n_p + n_s
    merged_rows = n_tok + 2 * MOE_TILE
    tile_p = min(MIXER_TILE, tp)
    tile_s = min(MIXER_TILE, ts)
    assert n_p % tile_s == 0

    p = jax.nn.softmax(lb_logits.astype(F32), axis=0)
    lbs = jnp.cumsum(p, axis=0) - p[0]

    row = lambda a: a.reshape(1, -1).astype(F32)
    wr = jnp.zeros((N_EXPERTS, d_model), F32).at[:, :].set(w_router.T.astype(F32))
    wr_hi = wr.astype(BF16)
    wr_lo = (wr - wr_hi.astype(F32)).astype(BF16)
    rb = router_bias.astype(F32).reshape(N_EXPERTS, 1)

    hist_pad = HIST_ROWS - (CONV_K - 1)
    pad_hist = lambda a: jnp.pad(a, ((0, 0), (hist_pad, 0), (0, 0)))

    xp = x_prompt.reshape(n_p, d_model)
    xs = x_sample.reshape(n_s, d_model)
    hg_p, cv_p, hg_s, cv_s = [], [], [], []
    cur = None
    for l in range(depth):
        wl = w_in[l]
        cols = lambda a, b: wl[:, a:b].astype(BF16)
        o = 0
        mw = {"lb": row(lbs[l]), "lnin_g": row(ln_in_g), "lnin_b": row(ln_in_b)}
        for name, width in (("wq", d_a), ("wf", d_a), ("wv", d_a), ("wg", d_a), ("wua", d_b), ("wub", d_b),
                            ("wga", d_model), ("wgb", d_model)):
            mw[name] = cols(o, o + width)
            o += width
        mw.update({
            "hgn": row(hg_norm_g[l]), "woa": w_o_a[l].astype(BF16),
            "cw": jnp.pad(conv_w[l].astype(F32), ((0, HIST_ROWS - CONV_K), (0, 0))),
            "cb": row(conv_b[l]), "cng": row(cn_g[l]), "cnb": row(cn_b[l]),
            "wob": w_o_b[l].astype(BF16), "wout": w_out[l].astype(BF16),
            "ln1g": row(ln1_g[l]), "ln1b": row(ln1_b[l]), "wr_hi": wr_hi, "wr_lo": wr_lo, "rb": rb,
        })
        if l == 0:
            src_p, blk_p, src_s, blk_s = xp, 0, xs, 0
        else:
            src_p, blk_p, src_s, blk_s = cur, 0, cur, n_p // tile_s
        zeros_hg = jnp.zeros((bp, heads, dk, dk), F32)
        zeros_cv = jnp.zeros((bp, HIST_ROWS, d_b), F32)
        merged, route_p, hgp, cvp = _mixer_call(
            src_p, blk_p, bp, tp, zeros_hg, zeros_cv, mw, entry_ln=(l == 0), alpha=alpha,
            merged_rows=merged_rows, merged_block0=0)
        merged, route_s, hgs, cvs = _mixer_call(
            src_s, blk_s, bs, ts, state_hgrn[l].astype(F32), pad_hist(state_conv[l].astype(F32)), mw,
            entry_ln=(l == 0), alpha=alpha, merged=merged, merged_rows=merged_rows,
            merged_block0=n_p // tile_s)
        route = jnp.concatenate([
            route_p.transpose(1, 0, 2).reshape(ROUTE_ROWS, n_p),
            route_s.transpose(1, 0, 2).reshape(ROUTE_ROWS, n_s)], axis=1)
        ew = {"wg": w_gate_e[l].astype(BF16), "wu": w_up_e[l].astype(BF16), "wd": w_down_e[l].astype(BF16),
              "ln2g": row(ln2_g[l]), "ln2b": row(ln2_b[l])}
        cur = _moe_call(merged, n_tok, route, ew, alpha=alpha)
        hg_p.append(hgp)
        cv_p.append(cvp[:, hist_pad:, :])
        hg_s.append(hgs)
        cv_s.append(cvs[:, hist_pad:, :])

    y_prompt = cur[:n_p].reshape(bp, tp, d_model)
    y_sample = cur[n_p:n_tok].reshape(bs, ts, d_model)
    return (y_prompt, y_sample, jnp.stack(hg_p), jnp.stack(cv_p), jnp.stack(hg_s), jnp.stack(cv_s))
```

```python
import functools
import math

import jax
import jax.numpy as jnp
from jax import lax
from jax.experimental import pallas as pl
from jax.experimental.pallas import tpu as pltpu

F32 = jnp.float32
BF16 = jnp.bfloat16

CONV_K = 31
N_EXPERTS = 16
N_GROUPS = 4
EXP_PER_GROUP = N_EXPERTS // N_GROUPS
N_KEYS = N_EXPERTS * N_EXPERTS
MASK_SCORE = -1.0e4
LN_EPS = 1e-5
LANES = 128
SUBLANES = 8
SUBLANE_SHIFT = 3
HIST_ROWS = 32
ROUTE_ROWS = 8
V7X_VMEM_LIMIT_BYTES = 60000 * 1024
MIXER_TILE = 256
HGRN_CHUNK = 128
PRECISE_TAIL = 256
PRECISE_TILE = 128
MOE_TILE = 256
FF_CHUNK = 256
IDX_BLOCK = 1024
IDX_WINDOW = 2 * IDX_BLOCK
IDX_SLOTS = 4


def _dot(a, b):
    return jnp.dot(a, b, preferred_element_type=F32)


def _dot_nt(a, b):
    return lax.dot_general(a, b, (((1,), (1,)), ((), ())), preferred_element_type=F32)


def _layer_norm(x, g, b):
    mu = jnp.mean(x, axis=-1, keepdims=True)
    xc = x - mu
    var = jnp.mean(xc * xc, axis=-1, keepdims=True)
    return xc * lax.rsqrt(var + LN_EPS) * g + b


def _split3(x):
    hi = x.astype(BF16)
    r1 = x - hi.astype(F32)
    mid = r1.astype(BF16)
    lo = (r1 - mid.astype(F32)).astype(BF16)
    return hi, mid, lo


def _neg_abs(x):
    bits = lax.bitcast_convert_type(x, jnp.uint32) | jnp.uint32(0x80000000)
    return lax.bitcast_convert_type(bits, F32)


def _level_weight(bc, h, rows):
    two_h = 2 * h
    if two_h >= SUBLANES:
        b3 = bc.reshape(rows // two_h, two_h, bc.shape[-1])
        ref = jnp.broadcast_to(b3[:, h - 1:h, :], b3.shape).reshape(bc.shape)
    else:
        u = lax.broadcasted_iota(jnp.int32, bc.shape, 0) & (two_h - 1)
        ref = bc
        for uu in range(two_h):
            delta = h - 1 - uu
            if delta == 0:
                continue
            shifted = pltpu.roll(bc, (-delta) % rows, 0)
            ref = jnp.where(u == uu, shifted, ref)
    return jnp.exp2(_neg_abs(bc - ref))


def _route_rows(logits_t, bias_t):
    rows = [logits_t[e:e + 1, :] for e in range(N_EXPERTS)]
    mx = functools.reduce(jnp.maximum, rows)
    ex = [jnp.exp(r - mx) for r in rows]
    den = functools.reduce(lambda a, b: a + b, ex)
    scores = [e_ / den for e_ in ex]
    sel = [scores[e] + bias_t[e:e + 1, :] for e in range(N_EXPERTS)]
    gscore = []
    for g in range(N_GROUPS):
        v = sel[g * EXP_PER_GROUP:(g + 1) * EXP_PER_GROUP]
        pairs = [v[i] + v[j] for i in range(EXP_PER_GROUP) for j in range(i + 1, EXP_PER_GROUP)]
        gscore.append(functools.reduce(jnp.maximum, pairs))
    best = gscore[0]
    gidx = jnp.zeros_like(best, dtype=jnp.int32)
    for g in range(1, N_GROUPS):
        upd = gscore[g] > best
        best = jnp.where(upd, gscore[g], best)
        gidx = jnp.where(upd, g, gidx)
    masked = [jnp.where(gidx == (e // EXP_PER_GROUP), sel[e], MASK_SCORE) for e in range(N_EXPERTS)]

    def top1(vals):
        bv = vals[0]
        bi = jnp.zeros_like(gidx)
        for e in range(1, N_EXPERTS):
            upd = vals[e] > bv
            bv = jnp.where(upd, vals[e], bv)
            bi = jnp.where(upd, e, bi)
        return bi

    i1 = top1(masked)
    i2 = top1([jnp.where(i1 == e, -jnp.inf, masked[e]) for e in range(N_EXPERTS)])
    w1 = jnp.zeros_like(best)
    w2 = jnp.zeros_like(best)
    for e in range(N_EXPERTS):
        w1 = jnp.where(i1 == e, scores[e], w1)
        w2 = jnp.where(i2 == e, scores[e], w2)
    tot = w1 + w2
    w1 = w1 / tot
    w2 = w2 / tot
    first_lo = i1 < i2
    key = jnp.minimum(i1, i2) * N_EXPERTS + jnp.maximum(i1, i2)
    return key.astype(F32), jnp.where(first_lo, w1, w2), jnp.where(first_lo, w2, w1)


MIXER_MATRICES = ("wq", "wf", "wv", "wg", "wua", "wub", "wga", "wgb", "woa", "wob", "wout")
MIXER_CONSTS = ("lb", "lnin_g", "lnin_b", "hgn", "cw", "cb", "cng", "cnb", "ln1g", "ln1b", "wr_hi", "wr_lo", "rb")
MIXER_OUTPUTS = ("x1", "route", "hgo", "cvo")
MIXER_SCRATCH = ("st", "ubuf", "ush", "q", "k", "v", "b", "o")


def _parts(a, precise):
    hi = a.astype(BF16)
    if not precise:
        return (hi,)
    return (hi, (a - hi.astype(F32)).astype(BF16))


def _mm(a, b, dot=_dot):
    acc = dot(a[0], b[0])
    if len(a) > 1:
        acc = acc + dot(a[1], b[0])
    if len(b) > 1:
        acc = acc + dot(a[0], b[1])
    return acc


def _mixer_kernel(*refs, names, precise, entry_ln, tile, chunk, heads, dk, d_model, alpha):
    r = dict(zip(names, refs))
    x_ref, hg_ref, cv_ref = r["x"], r["hg"], r["cv"]
    lnin_g, lnin_b, hgn_ref, cw_ref, cb_ref = r["lnin_g"], r["lnin_b"], r["hgn"], r["cw"], r["cb"]
    cng, cnb, ln1g, ln1b, wr_hi, wr_lo, rb_ref = r["cng"], r["cnb"], r["ln1g"], r["ln1b"], r["wr_hi"], r["wr_lo"], r["rb"]
    x1_ref, route_ref, hgo_ref, cvo_ref = r["x1"], r["route"], r["hgo"], r["cvo"]
    st_scr, ubuf, ush, q_scr, k_scr, v_scr, b_scr, o_scr = (r[n] for n in MIXER_SCRATCH)

    def weight(name):
        return (r[name][...], r[name + "_lo"][...]) if precise else (r[name][...],)

    t_id = pl.program_id(1)
    n_t = pl.num_programs(1)

    @pl.when(t_id == 0)
    def _():
        for h in range(heads):
            st_scr[h] = hg_ref[0, h].T
        ubuf[0:HIST_ROWS, :] = cv_ref[0]

    x = x_ref[...]
    if entry_ln:
        x = _layer_norm(x, lnin_g[...], lnin_b[...])
    xb = _parts(x, precise)

    lb = r["lb"][...]
    q_scr[...] = (_mm(xb, weight("wq")) * (dk ** -0.5)).astype(q_scr.dtype)
    fgate = lb + (1.0 - lb) * jax.nn.sigmoid(_mm(xb, weight("wf")))
    k_scr[...] = (1.0 - fgate).astype(k_scr.dtype)
    logf2 = jnp.log2(fgate)
    v_scr[...] = _mm(xb, weight("wv"))

    r_i = lax.broadcasted_iota(jnp.int32, (chunk, chunk), 0)
    c_i = lax.broadcasted_iota(jnp.int32, (chunk, chunk), 1)
    tri = jnp.where(c_i <= r_i, 1.0, 0.0).astype(BF16)
    n_chunks = tile // chunk
    for c in range(n_chunks):
        hi, mid, lo = _split3(logf2[c * chunk:(c + 1) * chunk, :])
        b_scr[c * chunk:(c + 1) * chunk, :] = (_dot(tri, hi) + _dot(tri, mid)) + _dot(tri, lo)

    n_levels = int(math.log2(chunk))
    xs = r_i ^ c_i
    lvl = jnp.full((chunk, chunk), -1, jnp.int32)
    for lv in range(n_levels):
        lvl = jnp.where((xs >> lv) == 1, lv, lvl)
    lvl = jnp.where(r_i >= c_i, lvl, n_levels)

    def head_body(h, carry):
        off = pl.multiple_of(h * dk, dk)
        st = st_scr[h]
        for c in range(n_chunks):
            rs = slice(c * chunk, (c + 1) * chunk)
            qc = q_scr[rs, pl.ds(off, dk)]
            kc = k_scr[rs, pl.ds(off, dk)]
            vc = v_scr[rs, pl.ds(off, dk)]
            bc = b_scr[rs, pl.ds(off, dk)]

            def scaled(a, w):
                return _parts(a * w, True) if precise else (a * w.astype(BF16),)

            o = _mm(scaled(qc, jnp.exp2(bc)), _parts(st, precise), _dot_nt)
            attn = jnp.where(lvl == -1, _mm(_parts(qc, precise), _parts(kc, precise), _dot_nt), 0.0)
            for lv in range(n_levels):
                w = _level_weight(bc, 1 << lv, chunk)
                attn = jnp.where(lvl == lv, _mm(scaled(qc, w), scaled(kc, w), _dot_nt), attn)
            o = o + _mm(_parts(attn, precise), _parts(vc, precise))
            bl = bc[chunk - 1:chunk, :]
            st = st * jnp.exp2(bl) + _mm(_parts(vc.T, precise), scaled(kc, jnp.exp2(bl - bc)))
            ms = jnp.mean(o * o, axis=-1, keepdims=True)
            o_scr[rs, pl.ds(off, dk)] = o * lax.rsqrt(ms + LN_EPS) * hgn_ref[...]
        st_scr[h] = st
        return carry

    lax.fori_loop(0, heads, head_body, 0, unroll=2)

    u = _mm(xb, weight("wua")) * jax.nn.sigmoid(_mm(xb, weight("wub")))
    ubuf[HIST_ROWS:HIST_ROWS + tile, :] = u
    sh_rows = tile + HIST_ROWS - SUBLANES
    for s in range(1, SUBLANES):
        ush[s - 1] = ubuf[s:s + sh_rows, :]
    base = HIST_ROWS - (CONV_K - 1)

    def taps(acc, j0, j1):
        for j in range(j0, j1):
            s = (base + j) % SUBLANES
            a = base + j - s
            win = ubuf[a:a + tile, :] if s == 0 else ush[s - 1, a:a + tile, :]
            term = win * cw_ref[j:j + 1, :]
            acc = term if acc is None else acc + term
        return acc

    acc = taps(None, 0, 8)
    g_o = _mm(xb, weight("wg"))
    acc = taps(acc, 8, 16)
    y_a = _mm(_parts(o_scr[...] * (g_o * jax.nn.sigmoid(g_o)), precise), weight("woa"))
    acc = taps(acc, 16, 24)
    gate_a = jax.nn.sigmoid(_mm(xb, weight("wga")))
    acc = taps(acc, 24, CONV_K)
    gate_b = jax.nn.sigmoid(_mm(xb, weight("wgb")))
    cn = _layer_norm(acc + cb_ref[...], cng[...], cnb[...])
    y_b = _mm(_parts(cn * jax.nn.sigmoid(cn), precise), weight("wob"))

    @pl.when(t_id == n_t - 1)
    def _():
        cvo_ref[0] = ubuf[tile:tile + HIST_ROWS, :]

    ubuf[0:HIST_ROWS, :] = ubuf[tile:tile + HIST_ROWS, :]

    m = gate_a * y_a + gate_b * y_b
    x1 = _layer_norm(alpha * x + _mm(_parts(m, precise), weight("wout")), ln1g[...], ln1b[...])
    x1_ref[:, 0:d_model] = x1

    x_hi = x1.astype(BF16)
    x_lo = (x1 - x_hi.astype(F32)).astype(BF16)
    logits_t = (_dot_nt(wr_hi[...], x_hi) + _dot_nt(wr_lo[...], x_hi)) + _dot_nt(wr_hi[...], x_lo)
    key, g_lo, g_hi = _route_rows(logits_t, rb_ref[...])
    route_ref[0] = jnp.broadcast_to(key, (ROUTE_ROWS, tile))
    row_id = lax.broadcasted_iota(jnp.int32, (LANES, tile), 0)
    g_rows = jnp.where(row_id == 0, g_lo, jnp.where(row_id == 1, g_hi, 0.0))
    x1_ref[:, d_model:d_model + LANES] = g_rows.T

    @pl.when(t_id == n_t - 1)
    def _():
        for h in range(heads):
            hgo_ref[0, h] = st_scr[h].T


def _const_spec(shape):
    nd = len(shape)
    return pl.BlockSpec(shape, lambda b, t, _n=nd: (0,) * _n, pipeline_mode=pl.Buffered(1))


def _mixer_call(x_rows, n_seq, seq_len, hg, cv, w, *, tile, x_row0, x_seq_rows, m_row0, m_seq_rows, merged_rows,
                precise, entry_ln, alpha, d_model, merged=None):
    heads, dk = hg.shape[1], hg.shape[2]
    d_a = heads * dk
    d_b = cv.shape[-1]
    chunk = min(HGRN_CHUNK, tile)
    n_t = seq_len // tile
    assert seq_len % tile == 0 and tile % chunk == 0 and tile >= HIST_ROWS
    assert x_row0 % tile == 0 and x_seq_rows % tile == 0 and m_row0 % tile == 0 and m_seq_rows % tile == 0
    xb0, xbs, mb0, mbs = x_row0 // tile, x_seq_rows // tile, m_row0 // tile, m_seq_rows // tile
    n_tiles = n_seq * n_t
    d_ext = d_model + LANES

    names = ["x", "hg", "cv"]
    in_specs = [
        pl.BlockSpec((tile, d_model), lambda b, t: (xb0 + b * xbs + t, 0)),
        pl.BlockSpec((1, heads, dk, dk), lambda b, t: (b, 0, 0, 0)),
        pl.BlockSpec((1, HIST_ROWS, d_b), lambda b, t: (b, 0, 0)),
    ]
    args = [x_rows, hg, cv]
    const_names = list(MIXER_CONSTS) + list(MIXER_MATRICES)
    if precise:
        const_names += [n + "_lo" for n in MIXER_MATRICES]
    for n in const_names:
        names.append(n)
        in_specs.append(_const_spec(w[n].shape))
        args.append(w[n])

    out_shape = [
        jax.ShapeDtypeStruct((merged_rows, d_ext), F32),
        jax.ShapeDtypeStruct((n_tiles, ROUTE_ROWS, tile), F32),
        jax.ShapeDtypeStruct(hg.shape, F32),
        jax.ShapeDtypeStruct(cv.shape, F32),
    ]
    out_specs = [
        pl.BlockSpec((tile, d_ext), lambda b, t: (mb0 + b * mbs + t, 0)),
        pl.BlockSpec((1, ROUTE_ROWS, tile), lambda b, t: (b * n_t + t, 0, 0)),
        pl.BlockSpec((1, heads, dk, dk), lambda b, t: (b, 0, 0, 0)),
        pl.BlockSpec((1, HIST_ROWS, d_b), lambda b, t: (b, 0, 0)),
    ]
    io_alias = {}
    if merged is not None:
        names.append("merged")
        in_specs.append(pl.BlockSpec(memory_space=pl.ANY))
        args.append(merged)
        io_alias = {len(args) - 1: 0}
    names += list(MIXER_OUTPUTS) + list(MIXER_SCRATCH)
    kern = functools.partial(_mixer_kernel, names=tuple(names), precise=precise, entry_ln=entry_ln, tile=tile,
                             chunk=chunk, heads=heads, dk=dk, d_model=d_model, alpha=alpha)

    qk_dtype = F32 if precise else BF16
    scratch = [
        pltpu.VMEM((heads, dk, dk), F32),
        pltpu.VMEM((HIST_ROWS + tile, d_b), F32),
        pltpu.VMEM((SUBLANES - 1, HIST_ROWS + tile - SUBLANES, d_b), F32),
        pltpu.VMEM((tile, d_a), qk_dtype),
        pltpu.VMEM((tile, d_a), qk_dtype),
        pltpu.VMEM((tile, d_a), F32),
        pltpu.VMEM((tile, d_a), F32),
        pltpu.VMEM((tile, d_a), F32),
    ]
    return pl.pallas_call(
        kern,
        grid=(n_seq, n_t),
        in_specs=in_specs,
        out_specs=out_specs,
        out_shape=out_shape,
        scratch_shapes=scratch,
        input_output_aliases=io_alias,
        compiler_params=pltpu.CompilerParams(
            dimension_semantics=("arbitrary", "arbitrary"),
            vmem_limit_bytes=V7X_VMEM_LIMIT_BYTES),
        name="mixer",
    )(*args)


def _moe_kernel(meta_ref, idx_hbm, x_hbm, wg1, wu1, wd1, wg2, wu2, wd2, ln2g, ln2b,
                out_hbm, idx_smem, xbuf, ybuf, xb_scr, acc_scr, gl_scr, gh_scr, isem, gsem, ssem,
                *, tile, n_tiles, n_tokens, d_model, d_ff, alpha):
    i = pl.program_id(0)
    n_used = meta_ref[0]
    slot = i % 2
    nslot = 1 - slot
    ff_chunk = min(FF_CHUNK, d_ff)
    n_fc = d_ff // ff_chunk
    n_groups_tile = tile // SUBLANES
    groups_per = n_groups_tile // n_fc

    def idx_copy(t):
        s = t % IDX_SLOTS
        start = pl.multiple_of((meta_ref[1 + t] // IDX_BLOCK) * IDX_BLOCK, IDX_BLOCK)
        return pltpu.make_async_copy(idx_hbm.at[pl.ds(start, IDX_WINDOW)],
                                     idx_smem.at[pl.ds(pl.multiple_of(s * IDX_WINDOW, IDX_WINDOW), IDX_WINDOW)],
                                     isem.at[s])

    def idx_base(t):
        return (t % IDX_SLOTS) * IDX_WINDOW + meta_ref[1 + t] % IDX_BLOCK

    def gather_rows(base, s, g0, n_groups):
        for gi in range(n_groups):
            for j in range(SUBLANES):
                tok = idx_smem[base + (g0 + gi) * SUBLANES + j]
                pltpu.make_async_copy(x_hbm.at[tok >> SUBLANE_SHIFT, pl.ds(tok & (SUBLANES - 1), 1)],
                                      xbuf.at[s, g0 + gi, pl.ds(j, 1)], gsem.at[s]).start()

    def scatter_rows(base, n_valid, s, g0, n_groups):
        for gi in range(n_groups):
            for j in range(SUBLANES):
                r = (g0 + gi) * SUBLANES + j
                dst = jnp.where(r < n_valid, idx_smem[base + r], n_tokens + s * tile + r)
                pltpu.make_async_copy(ybuf.at[s, g0 + gi, pl.ds(j, 1)],
                                      out_hbm.at[dst >> SUBLANE_SHIFT, pl.ds(dst & (SUBLANES - 1), 1)],
                                      ssem.at[s]).start()

    def wait_gather(s):
        pltpu.make_async_copy(x_hbm.at[pl.ds(0, n_groups_tile)], xbuf.at[s], gsem.at[s]).wait()

    def wait_scatter(s):
        pltpu.make_async_copy(ybuf.at[s], out_hbm.at[pl.ds(0, n_groups_tile)], ssem.at[s]).wait()

    @pl.when(i == 0)
    def _():
        ybuf[...] = jnp.zeros_like(ybuf)
        idx_copy(0).start()
        idx_copy(0).wait()
        base0 = idx_base(0)

        def body(g, c):
            gather_rows(base0, 0, g, 1)
            return c
        lax.fori_loop(0, n_groups_tile, body, 0)

        @pl.when(n_used > 1)
        def _():
            idx_copy(1).start()

    @pl.when(i < n_used)
    def _():
        @pl.when(i + 1 < n_used)
        def _():
            idx_copy(i + 1).wait()

        @pl.when(i + 2 < n_used)
        def _():
            idx_copy(i + 2).start()

        t_next = jnp.minimum(i + 1, n_used - 1)
        t_prev = jnp.maximum(i - 1, 0)
        base_next = idx_base(t_next)
        base_prev = idx_base(t_prev)
        n_valid_prev = jnp.where(i > 0, meta_ref[1 + n_tiles + t_prev], 0)

        wait_gather(slot)
        x = xbuf[slot, :, :, 0:d_model].reshape(tile, d_model)
        xb_scr[...] = x.astype(BF16)
        acc_scr[...] = alpha * x
        gates = xbuf[slot, :, :, d_model:d_model + LANES].reshape(tile, LANES)
        gl_scr[...] = jnp.broadcast_to(gates[:, 0:1], gl_scr.shape)
        gh_scr[...] = jnp.broadcast_to(gates[:, 1:2], gh_scr.shape)

        def expert_chunk(c, wg, wu, wd, gate_scr):
            fo = pl.multiple_of(c * ff_chunk, ff_chunk)
            xb = xb_scr[...]
            a = _dot(xb, wg[0, :, pl.ds(fo, ff_chunk)])
            u = _dot(xb, wu[0, :, pl.ds(fo, ff_chunk)])
            h = ((a * jax.nn.sigmoid(a)) * u) * gate_scr[...]
            acc_scr[...] += _dot(h.astype(BF16), wd[0, pl.ds(fo, ff_chunk), :])

        def body_lo(c, carry):
            gather_rows(base_next, nslot, c * groups_per, groups_per)
            expert_chunk(c, wg1, wu1, wd1, gl_scr)
            return carry
        lax.fori_loop(0, n_fc, body_lo, 0)

        def body_hi(c, carry):
            scatter_rows(base_prev, n_valid_prev, nslot, c * groups_per, groups_per)
            expert_chunk(c, wg2, wu2, wd2, gh_scr)
            return carry
        lax.fori_loop(0, n_fc, body_hi, 0)

        @pl.when(i >= 1)
        def _():
            wait_scatter(slot)

        ybuf[slot] = _layer_norm(acc_scr[...], ln2g[...], ln2b[...]).reshape(n_groups_tile, SUBLANES, d_model)

        @pl.when(i == n_used - 1)
        def _():
            base_cur = idx_base(i)
            n_valid_cur = meta_ref[1 + n_tiles + i]

            def body(g, c):
                scatter_rows(base_cur, n_valid_cur, slot, g, 1)
                return c
            lax.fori_loop(0, n_groups_tile, body, 0)
            wait_scatter(nslot)
            wait_scatter(slot)
            wait_gather(nslot)


def _moe_call(x_ext, n_tokens, key_f, w, *, d_model, alpha):
    tile = MOE_TILE
    n_pairs = N_EXPERTS * (N_EXPERTS - 1) // 2
    n_tiles = (n_tokens + n_pairs * (tile - 1)) // tile + 1

    key = key_f.astype(jnp.int32)
    _, order = lax.sort((key, lax.iota(jnp.int32, n_tokens)), num_keys=1)
    kid = lax.iota(jnp.int32, N_KEYS)
    counts = jnp.sum((key[:, None] == kid[None, :]).astype(jnp.int32), axis=0)
    starts = jnp.cumsum(counts) - counts
    k_tiles = (counts + tile - 1) // tile
    t_ends = jnp.cumsum(k_tiles)
    n_used = t_ends[-1]
    tid = lax.iota(jnp.int32, n_tiles)
    t_key = jnp.minimum(jnp.sum((t_ends[None, :] <= tid[:, None]).astype(jnp.int32), axis=1), N_KEYS - 1)
    onehot = (t_key[:, None] == kid[None, :]).astype(F32)
    table = jnp.stack([t_ends - k_tiles, starts, counts], axis=1).astype(F32)
    picked = jnp.dot(onehot, table, precision=lax.Precision.HIGHEST).astype(jnp.int32)
    m = tid - picked[:, 0]
    used = tid < n_used
    pos = jnp.where(used, picked[:, 1] + m * tile, 0)
    n_valid = jnp.where(used, jnp.clip(picked[:, 2] - m * tile, 0, tile), 0)
    last_key = jnp.sum(jnp.where(tid == n_used - 1, t_key, 0))
    t_key = jnp.where(used, t_key, last_key)
    meta = jnp.concatenate([n_used[None], pos, n_valid, t_key // N_EXPERTS, t_key % N_EXPERTS]).astype(jnp.int32)

    assert tile <= IDX_BLOCK
    idx_len = (-(-n_tokens // IDX_BLOCK) + 1) * IDX_BLOCK
    idx = jnp.pad(order, (0, idx_len - n_tokens))
    d_ff = w["wg"].shape[-1]
    ff_chunk = min(FF_CHUNK, d_ff)
    assert d_ff % ff_chunk == 0 and tile % (d_ff // ff_chunk) == 0

    def w_spec(which):
        def imap(i, meta_ref):
            return (meta_ref[1 + (2 + which) * n_tiles + i], 0, 0)
        return imap

    wshape_up = (1,) + w["wg"].shape[1:]
    wshape_dn = (1,) + w["wd"].shape[1:]
    d_ext = x_ext.shape[-1]
    grid_spec = pltpu.PrefetchScalarGridSpec(
        num_scalar_prefetch=1,
        grid=(n_tiles,),
        in_specs=[
            pl.BlockSpec(memory_space=pl.ANY),
            pl.BlockSpec(memory_space=pl.ANY),
            pl.BlockSpec(wshape_up, w_spec(0)),
            pl.BlockSpec(wshape_up, w_spec(0)),
            pl.BlockSpec(wshape_dn, w_spec(0)),
            pl.BlockSpec(wshape_up, w_spec(1)),
            pl.BlockSpec(wshape_up, w_spec(1)),
            pl.BlockSpec(wshape_dn, w_spec(1)),
            pl.BlockSpec((1, d_model), lambda i, m: (0, 0)),
            pl.BlockSpec((1, d_model), lambda i, m: (0, 0)),
        ],
        out_specs=pl.BlockSpec(memory_space=pl.ANY),
        scratch_shapes=[
            pltpu.SMEM((IDX_SLOTS * IDX_WINDOW,), jnp.int32),
            pltpu.VMEM((2, tile // SUBLANES, SUBLANES, d_ext), F32),
            pltpu.VMEM((2, tile // SUBLANES, SUBLANES, d_model), F32),
            pltpu.VMEM((tile, d_model), BF16),
            pltpu.VMEM((tile, d_model), F32),
            pltpu.VMEM((tile, ff_chunk), F32),
            pltpu.VMEM((tile, ff_chunk), F32),
            pltpu.SemaphoreType.DMA((IDX_SLOTS,)),
            pltpu.SemaphoreType.DMA((2,)),
            pltpu.SemaphoreType.DMA((2,)),
        ],
    )
    assert n_tokens % SUBLANES == 0 and tile % SUBLANES == 0
    out_rows = n_tokens + 2 * tile
    out = pl.pallas_call(
        functools.partial(_moe_kernel, tile=tile, n_tiles=n_tiles, n_tokens=n_tokens, d_model=d_model,
                          d_ff=d_ff, alpha=alpha),
        grid_spec=grid_spec,
        out_shape=jax.ShapeDtypeStruct((out_rows // SUBLANES, SUBLANES, d_model), F32),
        compiler_params=pltpu.CompilerParams(
            dimension_semantics=("arbitrary",),
            vmem_limit_bytes=V7X_VMEM_LIMIT_BYTES),
        name="moe",
    )(meta, idx, x_ext.reshape(n_tokens // SUBLANES, SUBLANES, d_ext), w["wg"], w["wu"], w["wd"],
      w["wg"], w["wu"], w["wd"], w["ln2g"], w["ln2b"])
    return out.reshape(out_rows, d_model)


def kernel(x_prompt, x_sample, state_hgrn, state_conv, ln_in_g, ln_in_b, w_in, lb_logits, hg_norm_g, w_o_a,
           conv_w, conv_b, cn_g, cn_b, w_o_b, w_out, ln1_g, ln1_b, w_router, router_bias, w_gate_e, w_up_e,
           w_down_e, ln2_g, ln2_b):
    depth = w_in.shape[0]
    bp, tp, d_model = x_prompt.shape
    bs, ts, _ = x_sample.shape
    heads, dk = state_hgrn.shape[2], state_hgrn.shape[3]
    d_a = heads * dk
    d_b = state_conv.shape[-1]
    alpha = float((2 * depth) ** 0.25)
    n_p, n_s = bp * tp, bs * ts
    n_tok = n_p + n_s
    tile_s = min(MIXER_TILE, ts)
    assert n_p % tile_s == 0

    p = jax.nn.softmax(lb_logits.astype(F32), axis=0)
    lbs = jnp.cumsum(p, axis=0) - p[0]

    row = lambda a: a.reshape(1, -1).astype(F32)
    wr = w_router.T.astype(F32)
    wr_hi = wr.astype(BF16)
    wr_lo = (wr - wr_hi.astype(F32)).astype(BF16)
    rb = router_bias.astype(F32).reshape(N_EXPERTS, 1)

    hist_pad = HIST_ROWS - (CONV_K - 1)
    pad_hist = lambda a: jnp.pad(a, ((0, 0), (hist_pad, 0), (0, 0)))

    xp = x_prompt.reshape(n_p, d_model)
    xs = x_sample.reshape(n_s, d_model)
    hg_p, cv_p, hg_s, cv_s = [], [], [], []
    cur = None
    for l in range(depth):
        tail_precise = l < depth - 1
        tail = min(PRECISE_TAIL, tp) if tail_precise else 0
        mats = {"woa": w_o_a[l], "wob": w_o_b[l], "wout": w_out[l]}
        o = 0
        for name, width in (("wq", d_a), ("wf", d_a), ("wv", d_a), ("wg", d_a), ("wua", d_b), ("wub", d_b),
                            ("wga", d_model), ("wgb", d_model)):
            mats[name] = w_in[l][:, o:o + width]
            o += width
        mw = {
            "lb": row(lbs[l]), "lnin_g": row(ln_in_g), "lnin_b": row(ln_in_b), "hgn": row(hg_norm_g[l]),
            "cw": jnp.pad(conv_w[l].astype(F32), ((0, HIST_ROWS - CONV_K), (0, 0))),
            "cb": row(conv_b[l]), "cng": row(cn_g[l]), "cnb": row(cn_b[l]),
            "ln1g": row(ln1_g[l]), "ln1b": row(ln1_b[l]), "wr_hi": wr_hi, "wr_lo": wr_lo, "rb": rb,
        }
        for name, m in mats.items():
            m = m.astype(F32)
            mw[name] = m.astype(BF16)
            if tail_precise:
                mw[name + "_lo"] = (m - mw[name].astype(F32)).astype(BF16)

        src_p, src_s, s_row0 = (xp, xs, 0) if l == 0 else (cur, cur, n_p)
        common = dict(entry_ln=(l == 0), alpha=alpha, d_model=d_model, merged_rows=n_tok)
        hgp = jnp.zeros((bp, heads, dk, dk), F32)
        cvp = jnp.zeros((bp, HIST_ROWS, d_b), F32)
        merged = None
        keys_p = []
        for seg_row0, seg_len, precise in ((0, tp - tail, False), (tp - tail, tail, tail_precise)):
            if seg_len == 0:
                continue
            tile = min(PRECISE_TILE if precise else MIXER_TILE, seg_len)
            merged, route, hgp, cvp = _mixer_call(
                src_p, bp, seg_len, hgp, cvp, mw, tile=tile, x_row0=seg_row0, x_seq_rows=tp, m_row0=seg_row0,
                m_seq_rows=tp, precise=precise, merged=merged, **common)
            keys_p.append(route[:, 0, :].reshape(bp, seg_len))
        merged, route_s, hgs, cvs = _mixer_call(
            src_s, bs, ts, state_hgrn[l].astype(F32), pad_hist(state_conv[l].astype(F32)), mw,
            tile=min(PRECISE_TILE if tail_precise else MIXER_TILE, ts), x_row0=s_row0, x_seq_rows=ts, m_row0=n_p,
            m_seq_rows=ts, precise=tail_precise, merged=merged, **common)
        key_f = jnp.concatenate([jnp.concatenate(keys_p, axis=1).reshape(n_p), route_s[:, 0, :].reshape(n_s)])
        ew = {"wg": w_gate_e[l].astype(BF16), "wu": w_up_e[l].astype(BF16), "wd": w_down_e[l].astype(BF16),
              "ln2g": row(ln2_g[l]), "ln2b": row(ln2_b[l])}
        cur = _moe_call(merged, n_tok, key_f, ew, d_model=d_model, alpha=alpha)
        hg_p.append(hgp)
        cv_p.append(cvp[:, hist_pad:, :])
        hg_s.append(hgs)
        cv_s.append(cvs[:, hist_pad:, :])

    y_prompt = cur[:n_p].reshape(bp, tp, d_model)
    y_sample = cur[n_p:n_tok].reshape(bs, ts, d_model)
    return (y_prompt, y_sample, jnp.stack(hg_p), jnp.stack(cv_p), jnp.stack(hg_s), jnp.stack(cv_s))
```

```python
import functools
import math

import jax
import jax.numpy as jnp
from jax import lax
from jax.experimental import pallas as pl
from jax.experimental.pallas import tpu as pltpu

F32 = jnp.float32
BF16 = jnp.bfloat16

CONV_K = 31
N_EXPERTS = 16
N_GROUPS = 4
EXP_PER_GROUP = N_EXPERTS // N_GROUPS
N_KEYS = N_EXPERTS * N_EXPERTS
MASK_SCORE = -1.0e4
LN_EPS = 1e-5
LANES = 128
SUBLANES = 8
SUBLANE_SHIFT = 3
HIST_ROWS = 32
ROUTE_ROWS = 8
V7X_VMEM_LIMIT_BYTES = 60000 * 1024
MIXER_TILE = 256
HGRN_CHUNK = 128
PRECISE_TAIL = 256
PRECISE_TILE = 128
MOE_TILE = 256
FF_CHUNK = 256
IDX_BLOCK = 1024
IDX_WINDOW = 2 * IDX_BLOCK
IDX_SLOTS = 4


def _dot(a, b):
    return jnp.dot(a, b, preferred_element_type=F32)


def _dot_nt(a, b):
    return lax.dot_general(a, b, (((1,), (1,)), ((), ())), preferred_element_type=F32)


def _layer_norm(x, g, b):
    mu = jnp.mean(x, axis=-1, keepdims=True)
    xc = x - mu
    var = jnp.mean(xc * xc, axis=-1, keepdims=True)
    return xc * lax.rsqrt(var + LN_EPS) * g + b


def _split3(x):
    hi = x.astype(BF16)
    r1 = x - hi.astype(F32)
    mid = r1.astype(BF16)
    lo = (r1 - mid.astype(F32)).astype(BF16)
    return hi, mid, lo


def _neg_abs(x):
    bits = lax.bitcast_convert_type(x, jnp.uint32) | jnp.uint32(0x80000000)
    return lax.bitcast_convert_type(bits, F32)


def _level_weight(bc, h, rows):
    two_h = 2 * h
    if two_h >= SUBLANES:
        b3 = bc.reshape(rows // two_h, two_h, bc.shape[-1])
        ref = jnp.broadcast_to(b3[:, h - 1:h, :], b3.shape).reshape(bc.shape)
    else:
        u = lax.broadcasted_iota(jnp.int32, bc.shape, 0) & (two_h - 1)
        ref = bc
        for uu in range(two_h):
            delta = h - 1 - uu
            if delta == 0:
                continue
            shifted = pltpu.roll(bc, (-delta) % rows, 0)
            ref = jnp.where(u == uu, shifted, ref)
    return jnp.exp2(_neg_abs(bc - ref))


def _route_rows(logits_t, bias_t):
    rows = [logits_t[e:e + 1, :] for e in range(N_EXPERTS)]
    mx = functools.reduce(jnp.maximum, rows)
    ex = [jnp.exp(r - mx) for r in rows]
    den = functools.reduce(lambda a, b: a + b, ex)
    scores = [e_ / den for e_ in ex]
    sel = [scores[e] + bias_t[e:e + 1, :] for e in range(N_EXPERTS)]
    gscore = []
    for g in range(N_GROUPS):
        v = sel[g * EXP_PER_GROUP:(g + 1) * EXP_PER_GROUP]
        pairs = [v[i] + v[j] for i in range(EXP_PER_GROUP) for j in range(i + 1, EXP_PER_GROUP)]
        gscore.append(functools.reduce(jnp.maximum, pairs))
    best = gscore[0]
    gidx = jnp.zeros_like(best, dtype=jnp.int32)
    for g in range(1, N_GROUPS):
        upd = gscore[g] > best
        best = jnp.where(upd, gscore[g], best)
        gidx = jnp.where(upd, g, gidx)
    masked = [jnp.where(gidx == (e // EXP_PER_GROUP), sel[e], MASK_SCORE) for e in range(N_EXPERTS)]

    def top1(vals):
        bv = vals[0]
        bi = jnp.zeros_like(gidx)
        for e in range(1, N_EXPERTS):
            upd = vals[e] > bv
            bv = jnp.where(upd, vals[e], bv)
            bi = jnp.where(upd, e, bi)
        return bi

    i1 = top1(masked)
    i2 = top1([jnp.where(i1 == e, -jnp.inf, masked[e]) for e in range(N_EXPERTS)])
    w1 = jnp.zeros_like(best)
    w2 = jnp.zeros_like(best)
    for e in range(N_EXPERTS):
        w1 = jnp.where(i1 == e, scores[e], w1)
        w2 = jnp.where(i2 == e, scores[e], w2)
    tot = w1 + w2
    w1 = w1 / tot
    w2 = w2 / tot
    first_lo = i1 < i2
    key = jnp.minimum(i1, i2) * N_EXPERTS + jnp.maximum(i1, i2)
    return key.astype(F32), jnp.where(first_lo, w1, w2), jnp.where(first_lo, w2, w1)


MIXER_MATRICES = ("wq", "wf", "wv", "wg", "wua", "wub", "wga", "wgb", "woa", "wob", "wout")
MIXER_CONSTS = ("lb", "lnin_g", "lnin_b", "hgn", "cw", "cb", "cng", "cnb", "ln1g", "ln1b", "wr_hi", "wr_lo", "rb")
MIXER_OUTPUTS = ("x1", "route", "hgo", "cvo")
MIXER_SCRATCH = ("st", "ubuf", "ush", "q", "k", "v", "b", "o")


def _parts(a, precise):
    hi = a.astype(BF16)
    if not precise:
        return (hi,)
    return (hi, (a - hi.astype(F32)).astype(BF16))


def _split_weight(w):
    w = w.astype(F32)
    hi = lax.bitcast_convert_type(lax.bitcast_convert_type(w, jnp.uint32) & jnp.uint32(0xFFFF0000), F32)
    return hi.astype(BF16), (w - hi).astype(BF16)


def _mm(a, b, dot=_dot):
    acc = dot(a[0], b[0])
    if len(a) > 1:
        acc = acc + dot(a[1], b[0])
    if len(b) > 1:
        acc = acc + dot(a[0], b[1])
    return acc


def _mixer_kernel(*refs, names, precise, entry_ln, tile, chunk, heads, dk, d_model, alpha):
    r = dict(zip(names, refs))
    x_ref, hg_ref, cv_ref = r["x"], r["hg"], r["cv"]
    lnin_g, lnin_b, hgn_ref, cw_ref, cb_ref = r["lnin_g"], r["lnin_b"], r["hgn"], r["cw"], r["cb"]
    cng, cnb, ln1g, ln1b, wr_hi, wr_lo, rb_ref = r["cng"], r["cnb"], r["ln1g"], r["ln1b"], r["wr_hi"], r["wr_lo"], r["rb"]
    x1_ref, route_ref, hgo_ref, cvo_ref = r["x1"], r["route"], r["hgo"], r["cvo"]
    st_scr, ubuf, ush, q_scr, k_scr, v_scr, b_scr, o_scr = (r[n] for n in MIXER_SCRATCH)

    def weight(name):
        return (r[name][...], r[name + "_lo"][...]) if precise else (r[name][...],)

    t_id = pl.program_id(1)
    n_t = pl.num_programs(1)

    @pl.when(t_id == 0)
    def _():
        for h in range(heads):
            st_scr[h] = hg_ref[0, h].T
        ubuf[0:HIST_ROWS, :] = cv_ref[0]

    x = x_ref[...]
    if entry_ln:
        x = _layer_norm(x, lnin_g[...], lnin_b[...])
    xb = _parts(x, precise)

    lb = r["lb"][...]
    q_scr[...] = (_mm(xb, weight("wq")) * (dk ** -0.5)).astype(q_scr.dtype)
    fgate = lb + (1.0 - lb) * jax.nn.sigmoid(_mm(xb, weight("wf")))
    k_scr[...] = (1.0 - fgate).astype(k_scr.dtype)
    logf2 = jnp.log2(fgate)
    v_scr[...] = _mm(xb, weight("wv"))

    r_i = lax.broadcasted_iota(jnp.int32, (chunk, chunk), 0)
    c_i = lax.broadcasted_iota(jnp.int32, (chunk, chunk), 1)
    tri = jnp.where(c_i <= r_i, 1.0, 0.0).astype(BF16)
    n_chunks = tile // chunk
    for c in range(n_chunks):
        hi, mid, lo = _split3(logf2[c * chunk:(c + 1) * chunk, :])
        b_scr[c * chunk:(c + 1) * chunk, :] = (_dot(tri, hi) + _dot(tri, mid)) + _dot(tri, lo)

    n_levels = int(math.log2(chunk))
    xs = r_i ^ c_i
    lvl = jnp.full((chunk, chunk), -1, jnp.int32)
    for lv in range(n_levels):
        lvl = jnp.where((xs >> lv) == 1, lv, lvl)
    lvl = jnp.where(r_i >= c_i, lvl, n_levels)

    def head_body(h, carry):
        off = pl.multiple_of(h * dk, dk)
        st = st_scr[h]
        for c in range(n_chunks):
            rs = slice(c * chunk, (c + 1) * chunk)
            qc = q_scr[rs, pl.ds(off, dk)]
            kc = k_scr[rs, pl.ds(off, dk)]
            vc = v_scr[rs, pl.ds(off, dk)]
            bc = b_scr[rs, pl.ds(off, dk)]

            def scaled(a, w):
                return _parts(a * w, True) if precise else (a * w.astype(BF16),)

            o = _mm(scaled(qc, jnp.exp2(bc)), _parts(st, precise), _dot_nt)
            attn = jnp.where(lvl == -1, _mm(_parts(qc, precise), _parts(kc, precise), _dot_nt), 0.0)
            for lv in range(n_levels):
                w = _level_weight(bc, 1 << lv, chunk)
                attn = jnp.where(lvl == lv, _mm(scaled(qc, w), scaled(kc, w), _dot_nt), attn)
            o = o + _mm(_parts(attn, precise), _parts(vc, precise))
            bl = bc[chunk - 1:chunk, :]
            st = st * jnp.exp2(bl) + _mm(_parts(vc.T, precise), scaled(kc, jnp.exp2(bl - bc)))
            ms = jnp.mean(o * o, axis=-1, keepdims=True)
            o_scr[rs, pl.ds(off, dk)] = o * lax.rsqrt(ms + LN_EPS) * hgn_ref[...]
        st_scr[h] = st
        return carry

    lax.fori_loop(0, heads, head_body, 0, unroll=2)

    u = _mm(xb, weight("wua")) * jax.nn.sigmoid(_mm(xb, weight("wub")))
    ubuf[HIST_ROWS:HIST_ROWS + tile, :] = u
    sh_rows = tile + HIST_ROWS - SUBLANES
    for s in range(1, SUBLANES):
        ush[s - 1] = ubuf[s:s + sh_rows, :]
    base = HIST_ROWS - (CONV_K - 1)

    def taps(acc, j0, j1):
        for j in range(j0, j1):
            s = (base + j) % SUBLANES
            a = base + j - s
            win = ubuf[a:a + tile, :] if s == 0 else ush[s - 1, a:a + tile, :]
            term = win * cw_ref[j:j + 1, :]
            acc = term if acc is None else acc + term
        return acc

    acc = taps(None, 0, 8)
    g_o = _mm(xb, weight("wg"))
    acc = taps(acc, 8, 16)
    y_a = _mm(_parts(o_scr[...] * (g_o * jax.nn.sigmoid(g_o)), precise), weight("woa"))
    acc = taps(acc, 16, 24)
    gate_a = jax.nn.sigmoid(_mm(xb, weight("wga")))
    acc = taps(acc, 24, CONV_K)
    gate_b = jax.nn.sigmoid(_mm(xb, weight("wgb")))
    cn = _layer_norm(acc + cb_ref[...], cng[...], cnb[...])
    y_b = _mm(_parts(cn * jax.nn.sigmoid(cn), precise), weight("wob"))

    @pl.when(t_id == n_t - 1)
    def _():
        cvo_ref[0] = ubuf[tile:tile + HIST_ROWS, :]

    ubuf[0:HIST_ROWS, :] = ubuf[tile:tile + HIST_ROWS, :]

    m = gate_a * y_a + gate_b * y_b
    x1 = _layer_norm(alpha * x + _mm(_parts(m, precise), weight("wout")), ln1g[...], ln1b[...])
    x1_ref[:, 0:d_model] = x1

    x_hi = x1.astype(BF16)
    x_lo = (x1 - x_hi.astype(F32)).astype(BF16)
    logits_t = (_dot_nt(wr_hi[...], x_hi) + _dot_nt(wr_lo[...], x_hi)) + _dot_nt(wr_hi[...], x_lo)
    key, g_lo, g_hi = _route_rows(logits_t, rb_ref[...])
    route_ref[0] = jnp.broadcast_to(key, (ROUTE_ROWS, tile))
    row_id = lax.broadcasted_iota(jnp.int32, (LANES, tile), 0)
    g_rows = jnp.where(row_id == 0, g_lo, jnp.where(row_id == 1, g_hi, 0.0))
    x1_ref[:, d_model:d_model + LANES] = g_rows.T

    @pl.when(t_id == n_t - 1)
    def _():
        for h in range(heads):
            hgo_ref[0, h] = st_scr[h].T


def _const_spec(shape):
    nd = len(shape)
    return pl.BlockSpec(shape, lambda b, t, _n=nd: (0,) * _n, pipeline_mode=pl.Buffered(1))


def _mixer_call(x_rows, n_seq, seq_len, hg, cv, w, *, tile, x_row0, x_seq_rows, m_row0, m_seq_rows, merged_rows,
                precise, entry_ln, alpha, d_model, merged=None):
    heads, dk = hg.shape[1], hg.shape[2]
    d_a = heads * dk
    d_b = cv.shape[-1]
    chunk = min(HGRN_CHUNK, tile)
    n_t = seq_len // tile
    assert seq_len % tile == 0 and tile % chunk == 0 and tile >= HIST_ROWS
    assert x_row0 % tile == 0 and x_seq_rows % tile == 0 and m_row0 % tile == 0 and m_seq_rows % tile == 0
    xb0, xbs, mb0, mbs = x_row0 // tile, x_seq_rows // tile, m_row0 // tile, m_seq_rows // tile
    n_tiles = n_seq * n_t
    d_ext = d_model + LANES

    names = ["x", "hg", "cv"]
    in_specs = [
        pl.BlockSpec((tile, d_model), lambda b, t: (xb0 + b * xbs + t, 0)),
        pl.BlockSpec((1, heads, dk, dk), lambda b, t: (b, 0, 0, 0)),
        pl.BlockSpec((1, HIST_ROWS, d_b), lambda b, t: (b, 0, 0)),
    ]
    args = [x_rows, hg, cv]
    const_names = list(MIXER_CONSTS) + list(MIXER_MATRICES)
    if precise:
        const_names += [n + "_lo" for n in MIXER_MATRICES]
    for n in const_names:
        names.append(n)
        in_specs.append(_const_spec(w[n].shape))
        args.append(w[n])

    out_shape = [
        jax.ShapeDtypeStruct((merged_rows, d_ext), F32),
        jax.ShapeDtypeStruct((n_tiles, ROUTE_ROWS, tile), F32),
        jax.ShapeDtypeStruct(hg.shape, F32),
        jax.ShapeDtypeStruct(cv.shape, F32),
    ]
    out_specs = [
        pl.BlockSpec((tile, d_ext), lambda b, t: (mb0 + b * mbs + t, 0)),
        pl.BlockSpec((1, ROUTE_ROWS, tile), lambda b, t: (b * n_t + t, 0, 0)),
        pl.BlockSpec((1, heads, dk, dk), lambda b, t: (b, 0, 0, 0)),
        pl.BlockSpec((1, HIST_ROWS, d_b), lambda b, t: (b, 0, 0)),
    ]
    io_alias = {}
    if merged is not None:
        names.append("merged")
        in_specs.append(pl.BlockSpec(memory_space=pl.ANY))
        args.append(merged)
        io_alias = {len(args) - 1: 0}
    names += list(MIXER_OUTPUTS) + list(MIXER_SCRATCH)
    kern = functools.partial(_mixer_kernel, names=tuple(names), precise=precise, entry_ln=entry_ln, tile=tile,
                             chunk=chunk, heads=heads, dk=dk, d_model=d_model, alpha=alpha)

    qk_dtype = F32 if precise else BF16
    scratch = [
        pltpu.VMEM((heads, dk, dk), F32),
        pltpu.VMEM((HIST_ROWS + tile, d_b), F32),
        pltpu.VMEM((SUBLANES - 1, HIST_ROWS + tile - SUBLANES, d_b), F32),
        pltpu.VMEM((tile, d_a), qk_dtype),
        pltpu.VMEM((tile, d_a), qk_dtype),
        pltpu.VMEM((tile, d_a), F32),
        pltpu.VMEM((tile, d_a), F32),
        pltpu.VMEM((tile, d_a), F32),
    ]
    return pl.pallas_call(
        kern,
        grid=(n_seq, n_t),
        in_specs=in_specs,
        out_specs=out_specs,
        out_shape=out_shape,
        scratch_shapes=scratch,
        input_output_aliases=io_alias,
        compiler_params=pltpu.CompilerParams(
            dimension_semantics=("arbitrary", "arbitrary"),
            vmem_limit_bytes=V7X_VMEM_LIMIT_BYTES),
        name="mixer",
    )(*args)


def _moe_kernel(meta_ref, idx_hbm, x_hbm, wg1, wu1, wd1, wg2, wu2, wd2, ln2g, ln2b,
                out_hbm, idx_smem, xbuf, ybuf, xb_scr, acc_scr, gl_scr, gh_scr, isem, gsem, ssem,
                *, tile, n_tiles, n_tokens, d_model, d_ff, alpha):
    i = pl.program_id(0)
    n_used = meta_ref[0]
    slot = i % 2
    nslot = 1 - slot
    ff_chunk = min(FF_CHUNK, d_ff)
    n_fc = d_ff // ff_chunk
    n_groups_tile = tile // SUBLANES
    groups_per = n_groups_tile // n_fc

    def idx_copy(t):
        s = t % IDX_SLOTS
        start = pl.multiple_of((meta_ref[1 + t] // IDX_BLOCK) * IDX_BLOCK, IDX_BLOCK)
        return pltpu.make_async_copy(idx_hbm.at[pl.ds(start, IDX_WINDOW)],
                                     idx_smem.at[pl.ds(pl.multiple_of(s * IDX_WINDOW, IDX_WINDOW), IDX_WINDOW)],
                                     isem.at[s])

    def idx_base(t):
        return (t % IDX_SLOTS) * IDX_WINDOW + meta_ref[1 + t] % IDX_BLOCK

    def gather_rows(base, s, g0, n_groups):
        for gi in range(n_groups):
            for j in range(SUBLANES):
                tok = idx_smem[base + (g0 + gi) * SUBLANES + j]
                pltpu.make_async_copy(x_hbm.at[tok >> SUBLANE_SHIFT, pl.ds(tok & (SUBLANES - 1), 1)],
                                      xbuf.at[s, g0 + gi, pl.ds(j, 1)], gsem.at[s]).start()

    def scatter_rows(base, n_valid, s, g0, n_groups):
        for gi in range(n_groups):
            for j in range(SUBLANES):
                r = (g0 + gi) * SUBLANES + j
                dst = jnp.where(r < n_valid, idx_smem[base + r], n_tokens + s * tile + r)
                pltpu.make_async_copy(ybuf.at[s, g0 + gi, pl.ds(j, 1)],
                                      out_hbm.at[dst >> SUBLANE_SHIFT, pl.ds(dst & (SUBLANES - 1), 1)],
                                      ssem.at[s]).start()

    def wait_gather(s):
        pltpu.make_async_copy(x_hbm.at[pl.ds(0, n_groups_tile)], xbuf.at[s], gsem.at[s]).wait()

    def wait_scatter(s):
        pltpu.make_async_copy(ybuf.at[s], out_hbm.at[pl.ds(0, n_groups_tile)], ssem.at[s]).wait()

    @pl.when(i == 0)
    def _():
        ybuf[...] = jnp.zeros_like(ybuf)
        idx_copy(0).start()
        idx_copy(0).wait()
        base0 = idx_base(0)

        def body(g, c):
            gather_rows(base0, 0, g, 1)
            return c
        lax.fori_loop(0, n_groups_tile, body, 0)

        @pl.when(n_used > 1)
        def _():
            idx_copy(1).start()

    @pl.when(i < n_used)
    def _():
        @pl.when(i + 1 < n_used)
        def _():
            idx_copy(i + 1).wait()

        @pl.when(i + 2 < n_used)
        def _():
            idx_copy(i + 2).start()

        t_next = jnp.minimum(i + 1, n_used - 1)
        t_prev = jnp.maximum(i - 1, 0)
        base_next = idx_base(t_next)
        base_prev = idx_base(t_prev)
        n_valid_prev = jnp.where(i > 0, meta_ref[1 + n_tiles + t_prev], 0)

        wait_gather(slot)
        x = xbuf[slot, :, :, 0:d_model].reshape(tile, d_model)
        xb_scr[...] = x.astype(BF16)
        acc_scr[...] = alpha * x
        gates = xbuf[slot, :, :, d_model:d_model + LANES].reshape(tile, LANES)
        gl_scr[...] = jnp.broadcast_to(gates[:, 0:1], gl_scr.shape)
        gh_scr[...] = jnp.broadcast_to(gates[:, 1:2], gh_scr.shape)

        def expert_chunk(c, wg, wu, wd, gate_scr):
            fo = pl.multiple_of(c * ff_chunk, ff_chunk)
            xb = xb_scr[...]
            a = _dot(xb, wg[0, :, pl.ds(fo, ff_chunk)])
            u = _dot(xb, wu[0, :, pl.ds(fo, ff_chunk)])
            h = ((a * jax.nn.sigmoid(a)) * u) * gate_scr[...]
            acc_scr[...] += _dot(h.astype(BF16), wd[0, pl.ds(fo, ff_chunk), :])

        def body_lo(c, carry):
            gather_rows(base_next, nslot, c * groups_per, groups_per)
            expert_chunk(c, wg1, wu1, wd1, gl_scr)
            return carry
        lax.fori_loop(0, n_fc, body_lo, 0)

        def body_hi(c, carry):
            scatter_rows(base_prev, n_valid_prev, nslot, c * groups_per, groups_per)
            expert_chunk(c, wg2, wu2, wd2, gh_scr)
            return carry
        lax.fori_loop(0, n_fc, body_hi, 0)

        @pl.when(i >= 1)
        def _():
            wait_scatter(slot)

        ybuf[slot] = _layer_norm(acc_scr[...], ln2g[...], ln2b[...]).reshape(n_groups_tile, SUBLANES, d_model)

        @pl.when(i == n_used - 1)
        def _():
            base_cur = idx_base(i)
            n_valid_cur = meta_ref[1 + n_tiles + i]

            def body(g, c):
                scatter_rows(base_cur, n_valid_cur, slot, g, 1)
                return c
            lax.fori_loop(0, n_groups_tile, body, 0)
            wait_scatter(nslot)
            wait_scatter(slot)
            wait_gather(nslot)


def _moe_call(x_ext, n_tokens, key_f, w, *, d_model, alpha):
    tile = MOE_TILE
    n_pairs = N_EXPERTS * (N_EXPERTS - 1) // 2
    n_tiles = (n_tokens + n_pairs * (tile - 1)) // tile + 1

    key = key_f.astype(jnp.int32)
    _, order = lax.sort((key, lax.iota(jnp.int32, n_tokens)), num_keys=1)
    kid = lax.iota(jnp.int32, N_KEYS)
    counts = jnp.sum((key[:, None] == kid[None, :]).astype(jnp.int32), axis=0)
    starts = jnp.cumsum(counts) - counts
    k_tiles = (counts + tile - 1) // tile
    t_ends = jnp.cumsum(k_tiles)
    n_used = t_ends[-1]
    tid = lax.iota(jnp.int32, n_tiles)
    t_key = jnp.minimum(jnp.sum((t_ends[None, :] <= tid[:, None]).astype(jnp.int32), axis=1), N_KEYS - 1)
    onehot = (t_key[:, None] == kid[None, :]).astype(F32)
    table = jnp.stack([t_ends - k_tiles, starts, counts], axis=1).astype(F32)
    picked = jnp.dot(onehot, table, precision=lax.Precision.HIGHEST).astype(jnp.int32)
    m = tid - picked[:, 0]
    used = tid < n_used
    pos = jnp.where(used, picked[:, 1] + m * tile, 0)
    n_valid = jnp.where(used, jnp.clip(picked[:, 2] - m * tile, 0, tile), 0)
    last_key = jnp.sum(jnp.where(tid == n_used - 1, t_key, 0))
    t_key = jnp.where(used, t_key, last_key)
    meta = jnp.concatenate([n_used[None], pos, n_valid, t_key // N_EXPERTS, t_key % N_EXPERTS]).astype(jnp.int32)

    assert tile <= IDX_BLOCK
    idx_len = (-(-n_tokens // IDX_BLOCK) + 1) * IDX_BLOCK
    idx = jnp.pad(order, (0, idx_len - n_tokens))
    d_ff = w["wg"].shape[-1]
    ff_chunk = min(FF_CHUNK, d_ff)
    assert d_ff % ff_chunk == 0 and tile % (d_ff // ff_chunk) == 0

    def w_spec(which):
        def imap(i, meta_ref):
            return (meta_ref[1 + (2 + which) * n_tiles + i], 0, 0)
        return imap

    wshape_up = (1,) + w["wg"].shape[1:]
    wshape_dn = (1,) + w["wd"].shape[1:]
    d_ext = x_ext.shape[-1]
    grid_spec = pltpu.PrefetchScalarGridSpec(
        num_scalar_prefetch=1,
        grid=(n_tiles,),
        in_specs=[
            pl.BlockSpec(memory_space=pl.ANY),
            pl.BlockSpec(memory_space=pl.ANY),
            pl.BlockSpec(wshape_up, w_spec(0)),
            pl.BlockSpec(wshape_up, w_spec(0)),
            pl.BlockSpec(wshape_dn, w_spec(0)),
            pl.BlockSpec(wshape_up, w_spec(1)),
            pl.BlockSpec(wshape_up, w_spec(1)),
            pl.BlockSpec(wshape_dn, w_spec(1)),
            pl.BlockSpec((1, d_model), lambda i, m: (0, 0)),
            pl.BlockSpec((1, d_model), lambda i, m: (0, 0)),
        ],
        out_specs=pl.BlockSpec(memory_space=pl.ANY),
        scratch_shapes=[
            pltpu.SMEM((IDX_SLOTS * IDX_WINDOW,), jnp.int32),
            pltpu.VMEM((2, tile // SUBLANES, SUBLANES, d_ext), F32),
            pltpu.VMEM((2, tile // SUBLANES, SUBLANES, d_model), F32),
            pltpu.VMEM((tile, d_model), BF16),
            pltpu.VMEM((tile, d_model), F32),
            pltpu.VMEM((tile, ff_chunk), F32),
            pltpu.VMEM((tile, ff_chunk), F32),
            pltpu.SemaphoreType.DMA((IDX_SLOTS,)),
            pltpu.SemaphoreType.DMA((2,)),
            pltpu.SemaphoreType.DMA((2,)),
        ],
    )
    assert n_tokens % SUBLANES == 0 and tile % SUBLANES == 0
    out_rows = n_tokens + 2 * tile
    out = pl.pallas_call(
        functools.partial(_moe_kernel, tile=tile, n_tiles=n_tiles, n_tokens=n_tokens, d_model=d_model,
                          d_ff=d_ff, alpha=alpha),
        grid_spec=grid_spec,
        out_shape=jax.ShapeDtypeStruct((out_rows // SUBLANES, SUBLANES, d_model), F32),
        compiler_params=pltpu.CompilerParams(
            dimension_semantics=("arbitrary",),
            vmem_limit_bytes=V7X_VMEM_LIMIT_BYTES),
        name="moe",
    )(meta, idx, x_ext.reshape(n_tokens // SUBLANES, SUBLANES, d_ext), w["wg"], w["wu"], w["wd"],
      w["wg"], w["wu"], w["wd"], w["ln2g"], w["ln2b"])
    return out.reshape(out_rows, d_model)


def kernel(x_prompt, x_sample, state_hgrn, state_conv, ln_in_g, ln_in_b, w_in, lb_logits, hg_norm_g, w_o_a,
           conv_w, conv_b, cn_g, cn_b, w_o_b, w_out, ln1_g, ln1_b, w_router, router_bias, w_gate_e, w_up_e,
           w_down_e, ln2_g, ln2_b):
    depth = w_in.shape[0]
    bp, tp, d_model = x_prompt.shape
    bs, ts, _ = x_sample.shape
    heads, dk = state_hgrn.shape[2], state_hgrn.shape[3]
    d_a = heads * dk
    d_b = state_conv.shape[-1]
    alpha = float((2 * depth) ** 0.25)
    n_p, n_s = bp * tp, bs * ts
    n_tok = n_p + n_s
    tile_s = min(MIXER_TILE, ts)
    assert n_p % tile_s == 0

    p = jax.nn.softmax(lb_logits.astype(F32), axis=0)
    lbs = jnp.cumsum(p, axis=0) - p[0]

    row = lambda a: a.reshape(1, -1).astype(F32)
    wr_hi, wr_lo = _split_weight(w_router.T)
    rb = router_bias.astype(F32).reshape(N_EXPERTS, 1)

    hist_pad = HIST_ROWS - (CONV_K - 1)
    pad_hist = lambda a: jnp.pad(a, ((0, 0), (hist_pad, 0), (0, 0)))

    xp = x_prompt.reshape(n_p, d_model)
    xs = x_sample.reshape(n_s, d_model)
    hg_p, cv_p, hg_s, cv_s = [], [], [], []
    cur = None
    for l in range(depth):
        tail_precise = l < depth - 1
        tail = min(PRECISE_TAIL, tp) if tail_precise else 0
        mats = {"woa": w_o_a[l], "wob": w_o_b[l], "wout": w_out[l]}
        o = 0
        for name, width in (("wq", d_a), ("wf", d_a), ("wv", d_a), ("wg", d_a), ("wua", d_b), ("wub", d_b),
                            ("wga", d_model), ("wgb", d_model)):
            mats[name] = w_in[l][:, o:o + width]
            o += width
        mw = {
            "lb": row(lbs[l]), "lnin_g": row(ln_in_g), "lnin_b": row(ln_in_b), "hgn": row(hg_norm_g[l]),
            "cw": jnp.pad(conv_w[l].astype(F32), ((0, HIST_ROWS - CONV_K), (0, 0))),
            "cb": row(conv_b[l]), "cng": row(cn_g[l]), "cnb": row(cn_b[l]),
            "ln1g": row(ln1_g[l]), "ln1b": row(ln1_b[l]), "wr_hi": wr_hi, "wr_lo": wr_lo, "rb": rb,
        }
        mw_precise = dict(mw)
        for name, m in mats.items():
            mw[name] = m.astype(BF16)
            if tail_precise:
                mw_precise[name], mw_precise[name + "_lo"] = _split_weight(m)

        src_p, src_s, s_row0 = (xp, xs, 0) if l == 0 else (cur, cur, n_p)
        common = dict(entry_ln=(l == 0), alpha=alpha, d_model=d_model, merged_rows=n_tok)
        hgp = jnp.zeros((bp, heads, dk, dk), F32)
        cvp = jnp.zeros((bp, HIST_ROWS, d_b), F32)
        merged = None
        keys_p = []
        for seg_row0, seg_len, precise in ((0, tp - tail, False), (tp - tail, tail, tail_precise)):
            if seg_len == 0:
                continue
            tile = min(PRECISE_TILE if precise else MIXER_TILE, seg_len)
            merged, route, hgp, cvp = _mixer_call(
                src_p, bp, seg_len, hgp, cvp, mw_precise if precise else mw, tile=tile, x_row0=seg_row0, x_seq_rows=tp, m_row0=seg_row0,
                m_seq_rows=tp, precise=precise, merged=merged, **common)
            keys_p.append(route[:, 0, :].reshape(bp, seg_len))
        merged, route_s, hgs, cvs = _mixer_call(
            src_s, bs, ts, state_hgrn[l].astype(F32), pad_hist(state_conv[l].astype(F32)),
            mw_precise if tail_precise else mw,
            tile=min(PRECISE_TILE if tail_precise else MIXER_TILE, ts), x_row0=s_row0, x_seq_rows=ts, m_row0=n_p,
            m_seq_rows=ts, precise=tail_precise, merged=merged, **common)
        key_f = jnp.concatenate([jnp.concatenate(keys_p, axis=1).reshape(n_p), route_s[:, 0, :].reshape(n_s)])
        ew = {"wg": w_gate_e[l].astype(BF16), "wu": w_up_e[l].astype(BF16), "wd": w_down_e[l].astype(BF16),
              "ln2g": row(ln2_g[l]), "ln2b": row(ln2_b[l])}
        cur = _moe_call(merged, n_tok, key_f, ew, d_model=d_model, alpha=alpha)
        hg_p.append(hgp)
        cv_p.append(cvp[:, hist_pad:, :])
        hg_s.append(hgs)
        cv_s.append(cvs[:, hist_pad:, :])

    y_prompt = cur[:n_p].reshape(bp, tp, d_model)
    y_sample = cur[n_p:n_tok].reshape(bs, ts, d_model)
    return (y_prompt, y_sample, jnp.stack(hg_p), jnp.stack(cv_p), jnp.stack(hg_s), jnp.stack(cv_s))
```

```python
import functools
import math

import jax
import jax.numpy as jnp
from jax import lax
from jax.experimental import pallas as pl
from jax.experimental.pallas import tpu as pltpu

F32 = jnp.float32
BF16 = jnp.bfloat16

CONV_K = 31
N_EXPERTS = 16
N_GROUPS = 4
EXP_PER_GROUP = N_EXPERTS // N_GROUPS
N_KEYS = N_EXPERTS * N_EXPERTS
MASK_SCORE = -1.0e4
LN_EPS = 1e-5
LANES = 128
SUBLANES = 8
SUBLANE_SHIFT = 3
HIST_ROWS = 32
ROUTE_ROWS = 8
V7X_VMEM_LIMIT_BYTES = 60000 * 1024
MIXER_TILE = 256
HGRN_CHUNK = 128
PRECISE_TAIL = 256
PRECISE_TILE = 128
MOE_TILE = 256
FF_CHUNK = 256
IDX_BLOCK = 1024
IDX_WINDOW = 2 * IDX_BLOCK
IDX_SLOTS = 4


def _dot(a, b):
    return jnp.dot(a, b, preferred_element_type=F32)


def _dot_nt(a, b):
    return lax.dot_general(a, b, (((1,), (1,)), ((), ())), preferred_element_type=F32)


def _layer_norm(x, g, b):
    mu = jnp.mean(x, axis=-1, keepdims=True)
    xc = x - mu
    var = jnp.mean(xc * xc, axis=-1, keepdims=True)
    return xc * lax.rsqrt(var + LN_EPS) * g + b


def _split3(x):
    hi = x.astype(BF16)
    r1 = x - hi.astype(F32)
    mid = r1.astype(BF16)
    lo = (r1 - mid.astype(F32)).astype(BF16)
    return hi, mid, lo


def _neg_abs(x):
    bits = lax.bitcast_convert_type(x, jnp.uint32) | jnp.uint32(0x80000000)
    return lax.bitcast_convert_type(bits, F32)


def _level_weight(bc, h, rows):
    two_h = 2 * h
    if two_h >= SUBLANES:
        b3 = bc.reshape(rows // two_h, two_h, bc.shape[-1])
        ref = jnp.broadcast_to(b3[:, h - 1:h, :], b3.shape).reshape(bc.shape)
    else:
        u = lax.broadcasted_iota(jnp.int32, bc.shape, 0) & (two_h - 1)
        ref = bc
        for uu in range(two_h):
            delta = h - 1 - uu
            if delta == 0:
                continue
            shifted = pltpu.roll(bc, (-delta) % rows, 0)
            ref = jnp.where(u == uu, shifted, ref)
    return jnp.exp2(_neg_abs(bc - ref))


def _route_rows(logits_t, bias_t):
    rows = [logits_t[e:e + 1, :] for e in range(N_EXPERTS)]
    mx = functools.reduce(jnp.maximum, rows)
    ex = [jnp.exp(r - mx) for r in rows]
    den = functools.reduce(lambda a, b: a + b, ex)
    scores = [e_ / den for e_ in ex]
    sel = [scores[e] + bias_t[e:e + 1, :] for e in range(N_EXPERTS)]
    gscore = []
    for g in range(N_GROUPS):
        v = sel[g * EXP_PER_GROUP:(g + 1) * EXP_PER_GROUP]
        pairs = [v[i] + v[j] for i in range(EXP_PER_GROUP) for j in range(i + 1, EXP_PER_GROUP)]
        gscore.append(functools.reduce(jnp.maximum, pairs))
    best = gscore[0]
    gidx = jnp.zeros_like(best, dtype=jnp.int32)
    for g in range(1, N_GROUPS):
        upd = gscore[g] > best
        best = jnp.where(upd, gscore[g], best)
        gidx = jnp.where(upd, g, gidx)
    masked = [jnp.where(gidx == (e // EXP_PER_GROUP), sel[e], MASK_SCORE) for e in range(N_EXPERTS)]

    def top1(vals):
        bv = vals[0]
        bi = jnp.zeros_like(gidx)
        for e in range(1, N_EXPERTS):
            upd = vals[e] > bv
            bv = jnp.where(upd, vals[e], bv)
            bi = jnp.where(upd, e, bi)
        return bi

    i1 = top1(masked)
    i2 = top1([jnp.where(i1 == e, -jnp.inf, masked[e]) for e in range(N_EXPERTS)])
    w1 = jnp.zeros_like(best)
    w2 = jnp.zeros_like(best)
    for e in range(N_EXPERTS):
        w1 = jnp.where(i1 == e, scores[e], w1)
        w2 = jnp.where(i2 == e, scores[e], w2)
    tot = w1 + w2
    w1 = w1 / tot
    w2 = w2 / tot
    first_lo = i1 < i2
    key = jnp.minimum(i1, i2) * N_EXPERTS + jnp.maximum(i1, i2)
    return key.astype(F32), jnp.where(first_lo, w1, w2), jnp.where(first_lo, w2, w1)


MIXER_MATRICES = ("wq", "wf", "wv", "wg", "wua", "wub", "wga", "wgb", "woa", "wob", "wout")
MIXER_CONSTS = ("lb", "lnin_g", "lnin_b", "hgn", "cw", "cb", "cng", "cnb", "ln1g", "ln1b", "wr", "rb")
MIXER_OUTPUTS = ("x1", "route", "hgo", "cvo")
MIXER_SCRATCH = ("st", "ubuf", "ush", "q", "k", "v", "b", "o")


def _parts(a, precise):
    hi = a.astype(BF16)
    if not precise:
        return (hi,)
    return (hi, (a - hi.astype(F32)).astype(BF16))


def _split_weight(w):
    w = w.astype(F32)
    hi = lax.bitcast_convert_type(lax.bitcast_convert_type(w, jnp.uint32) & jnp.uint32(0xFFFF0000), F32)
    return hi.astype(BF16), (w - hi).astype(BF16)


def _mm(a, b, dot=_dot):
    acc = dot(a[0], b[0])
    if len(a) > 1:
        acc = acc + dot(a[1], b[0])
    if len(b) > 1:
        acc = acc + dot(a[0], b[1])
    return acc


def _mixer_kernel(*refs, names, precise, entry_ln, tile, chunk, heads, dk, d_model, alpha):
    r = dict(zip(names, refs))
    x_ref, hg_ref, cv_ref = r["x"], r["hg"], r["cv"]
    lnin_g, lnin_b, hgn_ref, cw_ref, cb_ref = r["lnin_g"], r["lnin_b"], r["hgn"], r["cw"], r["cb"]
    cng, cnb, ln1g, ln1b, wr, rb_ref = r["cng"], r["cnb"], r["ln1g"], r["ln1b"], r["wr"], r["rb"]
    x1_ref, route_ref, hgo_ref, cvo_ref = r["x1"], r["route"], r["hgo"], r["cvo"]
    st_scr, ubuf, ush, q_scr, k_scr, v_scr, b_scr, o_scr = (r[n] for n in MIXER_SCRATCH)

    def weight(name):
        return (r[name][...], r[name + "_lo"][...]) if precise else (r[name][...],)

    t_id = pl.program_id(1)
    n_t = pl.num_programs(1)

    @pl.when(t_id == 0)
    def _():
        for h in range(heads):
            st_scr[h] = hg_ref[0, h].T
        ubuf[0:HIST_ROWS, :] = cv_ref[0]

    x = x_ref[...]
    if entry_ln:
        x = _layer_norm(x, lnin_g[...], lnin_b[...])
    xb = _parts(x, precise)

    lb = r["lb"][...]
    q_scr[...] = (_mm(xb, weight("wq")) * (dk ** -0.5)).astype(q_scr.dtype)
    fgate = lb + (1.0 - lb) * jax.nn.sigmoid(_mm(xb, weight("wf")))
    k_scr[...] = (1.0 - fgate).astype(k_scr.dtype)
    logf2 = jnp.log2(fgate)
    v_scr[...] = _mm(xb, weight("wv"))

    r_i = lax.broadcasted_iota(jnp.int32, (chunk, chunk), 0)
    c_i = lax.broadcasted_iota(jnp.int32, (chunk, chunk), 1)
    tri = jnp.where(c_i <= r_i, 1.0, 0.0).astype(BF16)
    n_chunks = tile // chunk
    for c in range(n_chunks):
        hi, mid, lo = _split3(logf2[c * chunk:(c + 1) * chunk, :])
        b_scr[c * chunk:(c + 1) * chunk, :] = (_dot(tri, hi) + _dot(tri, mid)) + _dot(tri, lo)

    n_levels = int(math.log2(chunk))
    xs = r_i ^ c_i
    lvl = jnp.full((chunk, chunk), -1, jnp.int32)
    for lv in range(n_levels):
        lvl = jnp.where((xs >> lv) == 1, lv, lvl)
    lvl = jnp.where(r_i >= c_i, lvl, n_levels)

    def head_body(h, carry):
        off = pl.multiple_of(h * dk, dk)
        st = st_scr[h]
        for c in range(n_chunks):
            rs = slice(c * chunk, (c + 1) * chunk)
            qc = q_scr[rs, pl.ds(off, dk)]
            kc = k_scr[rs, pl.ds(off, dk)]
            vc = v_scr[rs, pl.ds(off, dk)]
            bc = b_scr[rs, pl.ds(off, dk)]

            def scaled(a, w):
                return _parts(a * w, True) if precise else (a * w.astype(BF16),)

            o = _mm(scaled(qc, jnp.exp2(bc)), _parts(st, precise), _dot_nt)
            attn = jnp.where(lvl == -1, _mm(_parts(qc, precise), _parts(kc, precise), _dot_nt), 0.0)
            for lv in range(n_levels):
                w = _level_weight(bc, 1 << lv, chunk)
                attn = jnp.where(lvl == lv, _mm(scaled(qc, w), scaled(kc, w), _dot_nt), attn)
            o = o + _mm(_parts(attn, precise), _parts(vc, precise))
            bl = bc[chunk - 1:chunk, :]
            st = st * jnp.exp2(bl) + _mm(_parts(vc.T, precise), scaled(kc, jnp.exp2(bl - bc)))
            ms = jnp.mean(o * o, axis=-1, keepdims=True)
            o_scr[rs, pl.ds(off, dk)] = o * lax.rsqrt(ms + LN_EPS) * hgn_ref[...]
        st_scr[h] = st
        return carry

    lax.fori_loop(0, heads, head_body, 0, unroll=4)

    u = _mm(xb, weight("wua")) * jax.nn.sigmoid(_mm(xb, weight("wub")))
    ubuf[HIST_ROWS:HIST_ROWS + tile, :] = u
    sh_rows = tile + HIST_ROWS - SUBLANES
    for s in range(1, SUBLANES):
        ush[s - 1] = ubuf[s:s + sh_rows, :]
    base = HIST_ROWS - (CONV_K - 1)

    def taps(acc, j0, j1):
        for j in range(j0, j1):
            s = (base + j) % SUBLANES
            a = base + j - s
            win = ubuf[a:a + tile, :] if s == 0 else ush[s - 1, a:a + tile, :]
            term = win * cw_ref[j:j + 1, :]
            acc = term if acc is None else acc + term
        return acc

    acc = taps(None, 0, 8)
    g_o = _mm(xb, weight("wg"))
    acc = taps(acc, 8, 16)
    y_a = _mm(_parts(o_scr[...] * (g_o * jax.nn.sigmoid(g_o)), precise), weight("woa"))
    acc = taps(acc, 16, 24)
    gate_a = jax.nn.sigmoid(_mm(xb, weight("wga")))
    acc = taps(acc, 24, CONV_K)
    gate_b = jax.nn.sigmoid(_mm(xb, weight("wgb")))
    cn = _layer_norm(acc + cb_ref[...], cng[...], cnb[...])
    y_b = _mm(_parts(cn * jax.nn.sigmoid(cn), precise), weight("wob"))

    @pl.when(t_id == n_t - 1)
    def _():
        cvo_ref[0] = ubuf[tile:tile + HIST_ROWS, :]

    ubuf[0:HIST_ROWS, :] = ubuf[tile:tile + HIST_ROWS, :]

    m = gate_a * y_a + gate_b * y_b
    x1 = _layer_norm(alpha * x + _mm(_parts(m, precise), weight("wout")), ln1g[...], ln1b[...])
    x1_ref[:, 0:d_model] = x1

    x_hi = x1.astype(BF16)
    x_lo = (x1 - x_hi.astype(F32)).astype(BF16)
    logits_t = _dot_nt(wr[...], x_hi) + _dot_nt(wr[...], x_lo)
    key, g_lo, g_hi = _route_rows(logits_t, rb_ref[...])
    route_ref[0] = jnp.broadcast_to(key, (ROUTE_ROWS, tile))
    row_id = lax.broadcasted_iota(jnp.int32, (LANES, tile), 0)
    g_rows = jnp.where(row_id == 0, g_lo, jnp.where(row_id == 1, g_hi, 0.0))
    x1_ref[:, d_model:d_model + LANES] = g_rows.T

    @pl.when(t_id == n_t - 1)
    def _():
        for h in range(heads):
            hgo_ref[0, h] = st_scr[h].T


def _const_spec(shape):
    nd = len(shape)
    return pl.BlockSpec(shape, lambda b, t, _n=nd: (0,) * _n, pipeline_mode=pl.Buffered(1))


def _mixer_call(x_rows, n_seq, seq_len, hg, cv, w, *, tile, x_row0, x_seq_rows, m_row0, m_seq_rows, merged_rows,
                precise, entry_ln, alpha, d_model, merged=None):
    heads, dk = hg.shape[1], hg.shape[2]
    d_a = heads * dk
    d_b = cv.shape[-1]
    chunk = min(HGRN_CHUNK, tile)
    n_t = seq_len // tile
    assert seq_len % tile == 0 and tile % chunk == 0 and tile >= HIST_ROWS
    assert x_row0 % tile == 0 and x_seq_rows % tile == 0 and m_row0 % tile == 0 and m_seq_rows % tile == 0
    xb0, xbs, mb0, mbs = x_row0 // tile, x_seq_rows // tile, m_row0 // tile, m_seq_rows // tile
    n_tiles = n_seq * n_t
    d_ext = d_model + LANES

    names = ["x", "hg", "cv"]
    in_specs = [
        pl.BlockSpec((tile, d_model), lambda b, t: (xb0 + b * xbs + t, 0)),
        pl.BlockSpec((1, heads, dk, dk), lambda b, t: (b, 0, 0, 0)),
        pl.BlockSpec((1, HIST_ROWS, d_b), lambda b, t: (b, 0, 0)),
    ]
    args = [x_rows, hg, cv]
    const_names = list(MIXER_CONSTS) + list(MIXER_MATRICES)
    if precise:
        const_names += [n + "_lo" for n in MIXER_MATRICES]
    for n in const_names:
        names.append(n)
        in_specs.append(_const_spec(w[n].shape))
        args.append(w[n])

    out_shape = [
        jax.ShapeDtypeStruct((merged_rows, d_ext), F32),
        jax.ShapeDtypeStruct((n_tiles, ROUTE_ROWS, tile), F32),
        jax.ShapeDtypeStruct(hg.shape, F32),
        jax.ShapeDtypeStruct(cv.shape, F32),
    ]
    out_specs = [
        pl.BlockSpec((tile, d_ext), lambda b, t: (mb0 + b * mbs + t, 0)),
        pl.BlockSpec((1, ROUTE_ROWS, tile), lambda b, t: (b * n_t + t, 0, 0)),
        pl.BlockSpec((1, heads, dk, dk), lambda b, t: (b, 0, 0, 0)),
        pl.BlockSpec((1, HIST_ROWS, d_b), lambda b, t: (b, 0, 0)),
    ]
    io_alias = {}
    if merged is not None:
        names.append("merged")
        in_specs.append(pl.BlockSpec(memory_space=pl.ANY))
        args.append(merged)
        io_alias = {len(args) - 1: 0}
    names += list(MIXER_OUTPUTS) + list(MIXER_SCRATCH)
    kern = functools.partial(_mixer_kernel, names=tuple(names), precise=precise, entry_ln=entry_ln, tile=tile,
                             chunk=chunk, heads=heads, dk=dk, d_model=d_model, alpha=alpha)

    qk_dtype = F32 if precise else BF16
    scratch = [
        pltpu.VMEM((heads, dk, dk), F32),
        pltpu.VMEM((HIST_ROWS + tile, d_b), F32),
        pltpu.VMEM((SUBLANES - 1, HIST_ROWS + tile - SUBLANES, d_b), F32),
        pltpu.VMEM((tile, d_a), qk_dtype),
        pltpu.VMEM((tile, d_a), qk_dtype),
        pltpu.VMEM((tile, d_a), F32),
        pltpu.VMEM((tile, d_a), F32),
        pltpu.VMEM((tile, d_a), F32),
    ]
    return pl.pallas_call(
        kern,
        grid=(n_seq, n_t),
        in_specs=in_specs,
        out_specs=out_specs,
        out_shape=out_shape,
        scratch_shapes=scratch,
        input_output_aliases=io_alias,
        compiler_params=pltpu.CompilerParams(
            dimension_semantics=("arbitrary", "arbitrary"),
            vmem_limit_bytes=V7X_VMEM_LIMIT_BYTES),
        name="mixer",
    )(*args)


def _moe_kernel(meta_ref, idx_hbm, x_hbm, wg1, wu1, wd1, wg2, wu2, wd2, ln2g, ln2b,
                out_hbm, idx_smem, xbuf, ybuf, xb_scr, acc_scr, gl_scr, gh_scr, isem, gsem, ssem,
                *, tile, n_tiles, n_tokens, d_model, d_ff, alpha):
    i = pl.program_id(0)
    n_used = meta_ref[0]
    slot = i % 2
    nslot = 1 - slot
    ff_chunk = min(FF_CHUNK, d_ff)
    n_fc = d_ff // ff_chunk
    n_groups_tile = tile // SUBLANES
    groups_per = n_groups_tile // n_fc

    def idx_copy(t):
        s = t % IDX_SLOTS
        start = pl.multiple_of((meta_ref[1 + t] // IDX_BLOCK) * IDX_BLOCK, IDX_BLOCK)
        return pltpu.make_async_copy(idx_hbm.at[pl.ds(start, IDX_WINDOW)],
                                     idx_smem.at[pl.ds(pl.multiple_of(s * IDX_WINDOW, IDX_WINDOW), IDX_WINDOW)],
                                     isem.at[s])

    def idx_base(t):
        return (t % IDX_SLOTS) * IDX_WINDOW + meta_ref[1 + t] % IDX_BLOCK

    def gather_rows(base, s, g0, n_groups):
        for gi in range(n_groups):
            for j in range(SUBLANES):
                tok = idx_smem[base + (g0 + gi) * SUBLANES + j]
                pltpu.make_async_copy(x_hbm.at[tok >> SUBLANE_SHIFT, pl.ds(tok & (SUBLANES - 1), 1)],
                                      xbuf.at[s, g0 + gi, pl.ds(j, 1)], gsem.at[s]).start()

    def scatter_rows(base, n_valid, s, g0, n_groups):
        for gi in range(n_groups):
            for j in range(SUBLANES):
                r = (g0 + gi) * SUBLANES + j
                dst = jnp.where(r < n_valid, idx_smem[base + r], n_tokens + s * tile + r)
                pltpu.make_async_copy(ybuf.at[s, g0 + gi, pl.ds(j, 1)],
                                      out_hbm.at[dst >> SUBLANE_SHIFT, pl.ds(dst & (SUBLANES - 1), 1)],
                                      ssem.at[s]).start()

    def wait_gather(s):
        pltpu.make_async_copy(x_hbm.at[pl.ds(0, n_groups_tile)], xbuf.at[s], gsem.at[s]).wait()

    def wait_scatter(s):
        pltpu.make_async_copy(ybuf.at[s], out_hbm.at[pl.ds(0, n_groups_tile)], ssem.at[s]).wait()

    @pl.when(i == 0)
    def _():
        ybuf[...] = jnp.zeros_like(ybuf)
        idx_copy(0).start()
        idx_copy(0).wait()
        base0 = idx_base(0)

        def body(g, c):
            gather_rows(base0, 0, g, 1)
            return c
        lax.fori_loop(0, n_groups_tile, body, 0)

        @pl.when(n_used > 1)
        def _():
            idx_copy(1).start()

    @pl.when(i < n_used)
    def _():
        @pl.when(i + 1 < n_used)
        def _():
            idx_copy(i + 1).wait()

        @pl.when(i + 2 < n_used)
        def _():
            idx_copy(i + 2).start()

        t_next = jnp.minimum(i + 1, n_used - 1)
        t_prev = jnp.maximum(i - 1, 0)
        base_next = idx_base(t_next)
        base_prev = idx_base(t_prev)
        n_valid_prev = jnp.where(i > 0, meta_ref[1 + n_tiles + t_prev], 0)

        wait_gather(slot)
        x = xbuf[slot, :, :, 0:d_model].reshape(tile, d_model)
        xb_scr[...] = x.astype(BF16)
        acc_scr[...] = alpha * x
        gates = xbuf[slot, :, :, d_model:d_model + LANES].reshape(tile, LANES)
        gl_scr[...] = jnp.broadcast_to(gates[:, 0:1], gl_scr.shape)
        gh_scr[...] = jnp.broadcast_to(gates[:, 1:2], gh_scr.shape)

        def expert_chunk(c, wg, wu, wd, gate_scr):
            fo = pl.multiple_of(c * ff_chunk, ff_chunk)
            xb = xb_scr[...]
            a = _dot(xb, wg[0, :, pl.ds(fo, ff_chunk)])
            u = _dot(xb, wu[0, :, pl.ds(fo, ff_chunk)])
            h = ((a * jax.nn.sigmoid(a)) * u) * gate_scr[...]
            acc_scr[...] += _dot(h.astype(BF16), wd[0, pl.ds(fo, ff_chunk), :])

        def body_lo(c, carry):
            gather_rows(base_next, nslot, c * groups_per, groups_per)
            expert_chunk(c, wg1, wu1, wd1, gl_scr)
            return carry
        lax.fori_loop(0, n_fc, body_lo, 0)

        def body_hi(c, carry):
            scatter_rows(base_prev, n_valid_prev, nslot, c * groups_per, groups_per)
            expert_chunk(c, wg2, wu2, wd2, gh_scr)
            return carry
        lax.fori_loop(0, n_fc, body_hi, 0)

        @pl.when(i >= 1)
        def _():
            wait_scatter(slot)

        ybuf[slot] = _layer_norm(acc_scr[...], ln2g[...], ln2b[...]).reshape(n_groups_tile, SUBLANES, d_model)

        @pl.when(i == n_used - 1)
        def _():
            base_cur = idx_base(i)
            n_valid_cur = meta_ref[1 + n_tiles + i]

            def body(g, c):
                scatter_rows(base_cur, n_valid_cur, slot, g, 1)
                return c
            lax.fori_loop(0, n_groups_tile, body, 0)
            wait_scatter(nslot)
            wait_scatter(slot)
            wait_gather(nslot)


def _moe_call(x_ext, n_tokens, key_f, w, *, d_model, alpha):
    tile = MOE_TILE
    n_pairs = N_EXPERTS * (N_EXPERTS - 1) // 2
    n_tiles = (n_tokens + n_pairs * (tile - 1)) // tile + 1

    key = key_f.astype(jnp.int32)
    _, order = lax.sort((key, lax.iota(jnp.int32, n_tokens)), num_keys=1)
    kid = lax.iota(jnp.int32, N_KEYS)
    counts = jnp.sum((key[:, None] == kid[None, :]).astype(jnp.int32), axis=0)
    starts = jnp.cumsum(counts) - counts
    k_tiles = (counts + tile - 1) // tile
    t_ends = jnp.cumsum(k_tiles)
    n_used = t_ends[-1]
    tid = lax.iota(jnp.int32, n_tiles)
    t_key = jnp.minimum(jnp.sum((t_ends[None, :] <= tid[:, None]).astype(jnp.int32), axis=1), N_KEYS - 1)
    onehot = (t_key[:, None] == kid[None, :]).astype(F32)
    table = jnp.stack([t_ends - k_tiles, starts, counts], axis=1).astype(F32)
    picked = jnp.dot(onehot, table, precision=lax.Precision.HIGHEST).astype(jnp.int32)
    m = tid - picked[:, 0]
    used = tid < n_used
    pos = jnp.where(used, picked[:, 1] + m * tile, 0)
    n_valid = jnp.where(used, jnp.clip(picked[:, 2] - m * tile, 0, tile), 0)
    last_key = jnp.sum(jnp.where(tid == n_used - 1, t_key, 0))
    t_key = jnp.where(used, t_key, last_key)
    meta = jnp.concatenate([n_used[None], pos, n_valid, t_key // N_EXPERTS, t_key % N_EXPERTS]).astype(jnp.int32)

    assert tile <= IDX_BLOCK
    idx_len = (-(-n_tokens // IDX_BLOCK) + 1) * IDX_BLOCK
    idx = jnp.pad(order, (0, idx_len - n_tokens))
    d_ff = w["wg"].shape[-1]
    ff_chunk = min(FF_CHUNK, d_ff)
    assert d_ff % ff_chunk == 0 and tile % (d_ff // ff_chunk) == 0

    def w_spec(which):
        def imap(i, meta_ref):
            return (meta_ref[1 + (2 + which) * n_tiles + i], 0, 0)
        return imap

    wshape_up = (1,) + w["wg"].shape[1:]
    wshape_dn = (1,) + w["wd"].shape[1:]
    d_ext = x_ext.shape[-1]
    grid_spec = pltpu.PrefetchScalarGridSpec(
        num_scalar_prefetch=1,
        grid=(n_tiles,),
        in_specs=[
            pl.BlockSpec(memory_space=pl.ANY),
            pl.BlockSpec(memory_space=pl.ANY),
            pl.BlockSpec(wshape_up, w_spec(0)),
            pl.BlockSpec(wshape_up, w_spec(0)),
            pl.BlockSpec(wshape_dn, w_spec(0)),
            pl.BlockSpec(wshape_up, w_spec(1)),
            pl.BlockSpec(wshape_up, w_spec(1)),
            pl.BlockSpec(wshape_dn, w_spec(1)),
            pl.BlockSpec((1, d_model), lambda i, m: (0, 0)),
            pl.BlockSpec((1, d_model), lambda i, m: (0, 0)),
        ],
        out_specs=pl.BlockSpec(memory_space=pl.ANY),
        scratch_shapes=[
            pltpu.SMEM((IDX_SLOTS * IDX_WINDOW,), jnp.int32),
            pltpu.VMEM((2, tile // SUBLANES, SUBLANES, d_ext), F32),
            pltpu.VMEM((2, tile // SUBLANES, SUBLANES, d_model), F32),
            pltpu.VMEM((tile, d_model), BF16),
            pltpu.VMEM((tile, d_model), F32),
            pltpu.VMEM((tile, ff_chunk), F32),
            pltpu.VMEM((tile, ff_chunk), F32),
            pltpu.SemaphoreType.DMA((IDX_SLOTS,)),
            pltpu.SemaphoreType.DMA((2,)),
            pltpu.SemaphoreType.DMA((2,)),
        ],
    )
    assert n_tokens % SUBLANES == 0 and tile % SUBLANES == 0
    out_rows = n_tokens + 2 * tile
    out = pl.pallas_call(
        functools.partial(_moe_kernel, tile=tile, n_tiles=n_tiles, n_tokens=n_tokens, d_model=d_model,
                          d_ff=d_ff, alpha=alpha),
        grid_spec=grid_spec,
        out_shape=jax.ShapeDtypeStruct((out_rows // SUBLANES, SUBLANES, d_model), F32),
        compiler_params=pltpu.CompilerParams(
            dimension_semantics=("arbitrary",),
            vmem_limit_bytes=V7X_VMEM_LIMIT_BYTES),
        name="moe",
    )(meta, idx, x_ext.reshape(n_tokens // SUBLANES, SUBLANES, d_ext), w["wg"], w["wu"], w["wd"],
      w["wg"], w["wu"], w["wd"], w["ln2g"], w["ln2b"])
    return out.reshape(out_rows, d_model)


def kernel(x_prompt, x_sample, state_hgrn, state_conv, ln_in_g, ln_in_b, w_in, lb_logits, hg_norm_g, w_o_a,
           conv_w, conv_b, cn_g, cn_b, w_o_b, w_out, ln1_g, ln1_b, w_router, router_bias, w_gate_e, w_up_e,
           w_down_e, ln2_g, ln2_b):
    depth = w_in.shape[0]
    bp, tp, d_model = x_prompt.shape
    bs, ts, _ = x_sample.shape
    heads, dk = state_hgrn.shape[2], state_hgrn.shape[3]
    d_a = heads * dk
    d_b = state_conv.shape[-1]
    alpha = float((2 * depth) ** 0.25)
    n_p, n_s = bp * tp, bs * ts
    n_tok = n_p + n_s
    tile_s = min(MIXER_TILE, ts)
    assert n_p % tile_s == 0

    p = jax.nn.softmax(lb_logits.astype(F32), axis=0)
    lbs = jnp.cumsum(p, axis=0) - p[0]

    row = lambda a: a.reshape(1, -1).astype(F32)
    wr = w_router.T.astype(BF16)
    rb = router_bias.astype(F32).reshape(N_EXPERTS, 1)

    hist_pad = HIST_ROWS - (CONV_K - 1)
    pad_hist = lambda a: jnp.pad(a, ((0, 0), (hist_pad, 0), (0, 0)))

    xp = x_prompt.reshape(n_p, d_model)
    xs = x_sample.reshape(n_s, d_model)
    hg_p, cv_p, hg_s, cv_s = [], [], [], []
    cur = None
    for l in range(depth):
        tail_precise = l < depth - 1
        tail = min(PRECISE_TAIL, tp) if tail_precise else 0
        mats = {"woa": w_o_a[l], "wob": w_o_b[l], "wout": w_out[l]}
        o = 0
        for name, width in (("wq", d_a), ("wf", d_a), ("wv", d_a), ("wg", d_a), ("wua", d_b), ("wub", d_b),
                            ("wga", d_model), ("wgb", d_model)):
            mats[name] = w_in[l][:, o:o + width]
            o += width
        mw = {
            "lb": row(lbs[l]), "lnin_g": row(ln_in_g), "lnin_b": row(ln_in_b), "hgn": row(hg_norm_g[l]),
            "cw": jnp.pad(conv_w[l].astype(F32), ((0, HIST_ROWS - CONV_K), (0, 0))),
            "cb": row(conv_b[l]), "cng": row(cn_g[l]), "cnb": row(cn_b[l]),
            "ln1g": row(ln1_g[l]), "ln1b": row(ln1_b[l]), "wr": wr, "rb": rb,
        }
        mw_precise = dict(mw)
        for name, m in mats.items():
            mw[name] = m.astype(BF16)
            if tail_precise:
                mw_precise[name], mw_precise[name + "_lo"] = _split_weight(m)

        src_p, src_s, s_row0 = (xp, xs, 0) if l == 0 else (cur, cur, n_p)
        common = dict(entry_ln=(l == 0), alpha=alpha, d_model=d_model, merged_rows=n_tok)
        hgp = jnp.zeros((bp, heads, dk, dk), F32)
        cvp = jnp.zeros((bp, HIST_ROWS, d_b), F32)
        merged = None
        keys_p = []
        for seg_row0, seg_len, precise in ((0, tp - tail, False), (tp - tail, tail, tail_precise)):
            if seg_len == 0:
                continue
            tile = min(PRECISE_TILE if precise else MIXER_TILE, seg_len)
            merged, route, hgp, cvp = _mixer_call(
                src_p, bp, seg_len, hgp, cvp, mw_precise if precise else mw, tile=tile, x_row0=seg_row0, x_seq_rows=tp, m_row0=seg_row0,
                m_seq_rows=tp, precise=precise, merged=merged, **common)
            keys_p.append(route[:, 0, :].reshape(bp, seg_len))
        merged, route_s, hgs, cvs = _mixer_call(
            src_s, bs, ts, state_hgrn[l].astype(F32), pad_hist(state_conv[l].astype(F32)),
            mw_precise if tail_precise else mw,
            tile=min(PRECISE_TILE if tail_precise else MIXER_TILE, ts), x_row0=s_row0, x_seq_rows=ts, m_row0=n_p,
            m_seq_rows=ts, precise=tail_precise, merged=merged, **common)
        key_f = jnp.concatenate([jnp.concatenate(keys_p, axis=1).reshape(n_p), route_s[:, 0, :].reshape(n_s)])
        ew = {"wg": w_gate_e[l].astype(BF16), "wu": w_up_e[l].astype(BF16), "wd": w_down_e[l].astype(BF16),
              "ln2g": row(ln2_g[l]), "ln2b": row(ln2_b[l])}
        cur = _moe_call(merged, n_tok, key_f, ew, d_model=d_model, alpha=alpha)
        hg_p.append(hgp)
        cv_p.append(cvp[:, hist_pad:, :])
        hg_s.append(hgs)
        cv_s.append(cvs[:, hist_pad:, :])

    y_prompt = cur[:n_p].reshape(bp, tp, d_model)
    y_sample = cur[n_p:n_tok].reshape(bs, ts, d_model)
    return (y_prompt, y_sample, jnp.stack(hg_p), jnp.stack(cv_p), jnp.stack(hg_s), jnp.stack(cv_s))
```
